```python
import math
import jax
import jax.numpy as jnp
from jax import lax
import numpy as np

D_MODEL = 1024
BATCH = 2
SEQ = 8192
DEPTH = 1
DEC_BATCH = 32
DEC_SEQ = 8
PAST_LEN = 16384
PAGE_SIZE = 128

HEAD_DIM = 64
A_WIDTH = D_MODEL // 2
N_HEADS_A = A_WIDTH // HEAD_DIM
IDX_HEADS = 4
IDX_DIM = 64
IDX_SCALE = (IDX_HEADS * IDX_DIM) ** -0.5
ATT_SCALE = HEAD_DIM ** -0.5
TOPK_MAX = 256
Q_BLOCK = 128
REL_BUCKETS = 32
REL_MAX_DIST = 128
HEAD_B = 64
B_WIDTH = D_MODEL // 2
N_HEADS_B = B_WIDTH // HEAD_B
D_DECAY = 64
D_AAA = 64
D_GATE = 128
GN_EPS = 64e-5
D_FF = 4 * D_MODEL
RMS_EPS = 1e-6

RWKV_SPLIT = (B_WIDTH, D_DECAY, B_WIDTH, B_WIDTH, D_AAA, D_GATE)
RWKV_COLS = 3 * B_WIDTH + D_DECAY + D_AAA + D_GATE
RWKV_OFF = 3 * A_WIDTH + IDX_HEADS * IDX_DIM + IDX_DIM + IDX_HEADS
SPLIT_SIZES = (A_WIDTH, A_WIDTH, A_WIDTH, IDX_HEADS * IDX_DIM, IDX_DIM, IDX_HEADS, RWKV_COLS, D_MODEL, D_MODEL)
N_COLS = RWKV_OFF + RWKV_COLS + 2 * D_MODEL

kernel_name = 'hybrid_dsa_rwkv7_gated_step'


def _rmsnorm(x, g):
    xf = x.astype(jnp.float32)
    y = xf * lax.rsqrt(jnp.mean(xf * xf, axis=-1, keepdims=True) + RMS_EPS)
    return (y * g.astype(jnp.float32)).astype(x.dtype)


def _split(x, sizes):
    out, off = [], 0
    for s in sizes:
        out.append(x[..., off:off + s])
        off += s
    return out


def _rel_bucket(dist):
    max_exact = REL_BUCKETS // 2
    d_f = jnp.maximum(dist, max_exact).astype(jnp.float32)
    large = max_exact + (jnp.log(d_f / max_exact) / math.log(REL_MAX_DIST / max_exact)
                         * (REL_BUCKETS - max_exact)).astype(jnp.int32)
    large = jnp.minimum(large, REL_BUCKETS - 1)
    return jnp.where(dist < max_exact, dist, large)


def _indexer_scores(q_idx, w_idx, k_idx):
    dots = jnp.einsum('bthd,bsd->bths', q_idx, k_idx).astype(jnp.float32)
    w = w_idx.astype(jnp.float32) * IDX_SCALE
    return jnp.einsum('bths,bth->bts', jax.nn.relu(dots), w)


def _sparse_attention(q, k_sel, v_sel, dist, rel_bias):
    valid = dist >= 0
    bias = rel_bias.astype(jnp.float32)[_rel_bucket(jnp.maximum(dist, 0))]
    logits = jnp.einsum('bthd,btkhd->bthk', q, k_sel).astype(jnp.float32) * ATT_SCALE
    logits = logits + jnp.moveaxis(bias, -1, -2)
    logits = jnp.where(valid[:, :, None, :], logits, -jnp.inf)
    probs = jax.nn.softmax(logits, axis=-1)
    return jnp.einsum('bthk,btkhd->bthd', probs.astype(v_sel.dtype), v_sel)


def _gather_rows(rows, idx):
    return jax.vmap(lambda r, i: r[i])(rows, idx)


def _prompt_attention(q, k, v, q_idx, w_idx, k_idx, rel_bias):
    b, s = q.shape[0], q.shape[1]
    topk = min(TOPK_MAX, s // 4)
    key_pos = jnp.arange(s)

    def block(i):
        t0 = i * Q_BLOCK
        qb = lax.dynamic_slice_in_dim(q, t0, Q_BLOCK, axis=1)
        qib = lax.dynamic_slice_in_dim(q_idx, t0, Q_BLOCK, axis=1)
        wib = lax.dynamic_slice_in_dim(w_idx, t0, Q_BLOCK, axis=1)
        q_pos = t0 + jnp.arange(Q_BLOCK)
        scores = _indexer_scores(qib, wib, k_idx)
        causal = key_pos[None, :] <= q_pos[:, None]
        scores = jnp.where(causal[None], scores, -jnp.inf)
        _, sel = lax.top_k(scores, topk)
        dist = q_pos[None, :, None] - sel
        return _sparse_attention(qb, _gather_rows(k, sel), _gather_rows(v, sel), dist, rel_bias)

    out = lax.map(block, jnp.arange(s // Q_BLOCK))
    return jnp.moveaxis(out, 0, 1).reshape(b, s, N_HEADS_A, HEAD_DIM)


def _sample_attention(q, k, v, q_idx, w_idx, k_idx, cache_k, cache_v, cache_idx_k, page_table, rel_bias):
    b, t = q.shape[0], q.shape[1]
    n_pages = page_table.shape[1]
    past = n_pages * PAGE_SIZE
    n_keys = past + t
    topk = min(TOPK_MAX, n_keys // 4)
    past_idx_k = cache_idx_k[page_table].reshape(b, past, IDX_DIM)
    all_idx_k = jnp.concatenate([past_idx_k, k_idx.astype(past_idx_k.dtype)], axis=1)
    scores = _indexer_scores(q_idx, w_idx, all_idx_k)
    q_pos = past + jnp.arange(t)
    causal = jnp.arange(n_keys)[None, :] <= q_pos[:, None]
    scores = jnp.where(causal[None], scores, -jnp.inf)
    _, sel = lax.top_k(scores, topk)
    from_past = (sel < past)[..., None, None]
    past_sel = jnp.minimum(sel, past - 1)
    phys = jax.vmap(lambda pt, i: pt[i])(page_table, past_sel // PAGE_SIZE)
    slot = past_sel % PAGE_SIZE
    new_sel = jnp.clip(sel - past, 0, t - 1)
    k_sel = jnp.where(from_past, cache_k[phys, slot], _gather_rows(k, new_sel).astype(cache_k.dtype))
    v_sel = jnp.where(from_past, cache_v[phys, slot], _gather_rows(v, new_sel).astype(cache_v.dtype))
    dist = q_pos[None, :, None] - sel
    return _sparse_attention(q, k_sel, v_sel, dist, rel_bias)


def _rwkv_branch(cur, prev, wkv0, p):
    b, t = cur.shape[0], cur.shape[1]
    mixed = cur + (prev - cur) * p['rwkv_mu']
    r, wd, k, v, ad, gd = _split(mixed, RWKV_SPLIT)
    w_log = -jax.nn.softplus(-(p['rwkv_w0'] + jnp.tanh(wd) @ p['rwkv_w2'])) - 0.5
    decay = jnp.exp(-jnp.exp(w_log.astype(jnp.float32)))
    a = jax.nn.sigmoid(p['rwkv_a0'] + ad @ p['rwkv_a2'])
    g = jax.nn.sigmoid(gd) @ p['rwkv_g2']

    def heads(z):
        return z.astype(jnp.float32).reshape(b, t, N_HEADS_B, HEAD_B)

    kk = heads(k * p['rwkv_k_k'])
    kk = kk / jnp.maximum(jnp.linalg.norm(kk, axis=-1, keepdims=True), 1e-12)
    k = k * (1 + (a - 1) * p['rwkv_k_a'])
    rh, wh, kh, vh, ah = heads(r), heads(decay), heads(k), heads(v), heads(a)

    def step(state, inp):
        r_t, w_t, k_t, v_t, a_t, b_t = inp
        sa = jnp.einsum('bhvk,bhk->bhv', state, a_t)
        state = (state * w_t[:, :, None, :] + sa[..., None] * b_t[:, :, None, :]
                 + v_t[..., None] * k_t[:, :, None, :])
        return state, jnp.einsum('bhvk,bhk->bhv', state, r_t)

    xs = tuple(jnp.moveaxis(z, 1, 0) for z in (rh, wh, kh, vh, -kk, kk * ah))
    state, ys = lax.scan(step, wkv0.astype(jnp.float32), xs)
    y = jnp.moveaxis(ys, 0, 1)
    mean = jnp.mean(y, axis=-1, keepdims=True)
    var = jnp.mean(jnp.square(y - mean), axis=-1, keepdims=True)
    yn = ((y - mean) * lax.rsqrt(var + GN_EPS)).reshape(b, t, B_WIDTH)
    yn = yn * p['rwkv_ln_w'].astype(jnp.float32) + p['rwkv_ln_b'].astype(jnp.float32)
    bonus = jnp.sum(rh * kh * p['rwkv_r_k'].astype(jnp.float32), axis=-1, keepdims=True) * vh
    out = (yn + bonus.reshape(b, t, B_WIDTH)) * g.astype(jnp.float32)
    return out.astype(cur.dtype), state


def _layer(x, shift_row, wkv0, attend, p):
    b, t = x.shape[0], x.shape[1]
    h = _rmsnorm(x, p['norm_mix'])
    proj = h @ p['w_in']
    q, k, v, q_idx, k_idx, w_idx, rw, gate_a, gate_b = _split(proj, SPLIT_SIZES)
    q = q.reshape(b, t, N_HEADS_A, HEAD_DIM)
    k = k.reshape(b, t, N_HEADS_A, HEAD_DIM)
    v = v.reshape(b, t, N_HEADS_A, HEAD_DIM)
    q_idx = q_idx.reshape(b, t, IDX_HEADS, IDX_DIM)
    att = attend(q, k, v, q_idx, w_idx, k_idx).reshape(b, t, A_WIDTH)
    prev_row = shift_row @ p['w_in'][:, RWKV_OFF:RWKV_OFF + RWKV_COLS]
    rw_prev = jnp.concatenate([prev_row[:, None, :], rw[:, :-1]], axis=1)
    rwkv_y, wkv_new = _rwkv_branch(rw, rw_prev, wkv0, p)
    merged = (jax.nn.sigmoid(gate_a) * (att @ p['w_branch_a'])
              + jax.nn.sigmoid(gate_b) * (rwkv_y @ p['w_branch_b']))
    x = x + merged @ p['w_out']
    hm = _rmsnorm(x, p['norm_mlp'])
    x = x + jnp.square(jax.nn.relu(hm @ p['w_mlp_in'])) @ p['w_mlp_out']
    return x, k, v, k_idx, wkv_new, h[:, -1]


def setup_inputs(seed: int = 0) -> dict:
    key = jax.random.key(seed)
    ks = jax.random.split(key, 32)
    n_pages = PAST_LEN // PAGE_SIZE
    n_used = DEC_BATCH * n_pages
    n_pool = n_used + max(1, n_used // 4)

    def nrm(k, shape, scale):
        return jax.random.normal(k, shape, jnp.float32) * scale

    inp = {}
    inp['x_prompt'] = nrm(ks[0], (BATCH, SEQ, D_MODEL), 1.0)
    inp['x_sample'] = nrm(ks[1], (DEC_BATCH, DEC_SEQ, D_MODEL), 1.0)
    inp['cache_k'] = nrm(ks[2], (n_pool, PAGE_SIZE, N_HEADS_A, HEAD_DIM), 1.0)
    inp['cache_v'] = nrm(ks[3], (n_pool, PAGE_SIZE, N_HEADS_A, HEAD_DIM), 1.0)
    inp['cache_idx_k'] = nrm(ks[4], (n_pool, PAGE_SIZE, IDX_DIM), 1.0)
    inp['state_wkv'] = nrm(ks[5], (DEC_BATCH, N_HEADS_B, HEAD_B, HEAD_B), 0.3)
    inp['state_shift'] = nrm(ks[6], (DEC_BATCH, D_MODEL), 1.0)
    inp['page_table'] = jax.random.permutation(ks[7], n_pool)[:n_used].reshape(DEC_BATCH, n_pages).astype(jnp.int32)
    inp['rel_bias'] = nrm(ks[8], (REL_BUCKETS, N_HEADS_A), 0.5)
    inp['norm_mix'] = 1.0 + nrm(ks[9], (D_MODEL,), 0.05)
    inp['w_in'] = nrm(ks[10], (D_MODEL, N_COLS), D_MODEL ** -0.5)
    inp['rwkv_mu'] = jax.random.uniform(ks[11], (RWKV_COLS,), jnp.float32)
    inp['rwkv_w0'] = jax.random.uniform(ks[12], (B_WIDTH,), jnp.float32, -6.0, -1.0)
    inp['rwkv_w2'] = nrm(ks[13], (D_DECAY, B_WIDTH), 0.1 * D_DECAY ** -0.5)
    inp['rwkv_a0'] = nrm(ks[14], (B_WIDTH,), 0.3)
    inp['rwkv_a2'] = nrm(ks[15], (D_AAA, B_WIDTH), D_AAA ** -0.5)
    inp['rwkv_g2'] = nrm(ks[16], (D_GATE, B_WIDTH), D_GATE ** -0.5)
    inp['rwkv_k_k'] = 0.85 + nrm(ks[17], (B_WIDTH,), 0.05)
    inp['rwkv_k_a'] = 1.0 + nrm(ks[18], (B_WIDTH,), 0.05)
    inp['rwkv_r_k'] = nrm(ks[19], (N_HEADS_B, HEAD_B), 0.1)
    inp['rwkv_ln_w'] = 1.0 + nrm(ks[20], (B_WIDTH,), 0.05)
    inp['rwkv_ln_b'] = nrm(ks[21], (B_WIDTH,), 0.02)
    inp['w_branch_a'] = nrm(ks[22], (A_WIDTH, D_MODEL), A_WIDTH ** -0.5)
    inp['w_branch_b'] = nrm(ks[23], (B_WIDTH, D_MODEL), B_WIDTH ** -0.5)
    inp['w_out'] = nrm(ks[24], (D_MODEL, D_MODEL), D_MODEL ** -0.5)
    inp['norm_mlp'] = 1.0 + nrm(ks[25], (D_MODEL,), 0.05)
    inp['w_mlp_in'] = nrm(ks[26], (D_MODEL, D_FF), D_MODEL ** -0.5)
    inp['w_mlp_out'] = nrm(ks[27], (D_FF, D_MODEL), D_FF ** -0.5)
    inp['norm_final'] = 1.0 + nrm(ks[28], (D_MODEL,), 0.05)
    return inp


def reference(x_prompt, x_sample, cache_k, cache_v, cache_idx_k, state_wkv, state_shift, page_table,
              rel_bias, norm_mix, w_in, rwkv_mu, rwkv_w0, rwkv_w2, rwkv_a0, rwkv_a2, rwkv_g2,
              rwkv_k_k, rwkv_k_a, rwkv_r_k, rwkv_ln_w, rwkv_ln_b, w_branch_a, w_branch_b, w_out,
              norm_mlp, w_mlp_in, w_mlp_out, norm_final):
    p = dict(norm_mix=norm_mix, w_in=w_in, rwkv_mu=rwkv_mu, rwkv_w0=rwkv_w0, rwkv_w2=rwkv_w2,
             rwkv_a0=rwkv_a0, rwkv_a2=rwkv_a2, rwkv_g2=rwkv_g2, rwkv_k_k=rwkv_k_k, rwkv_k_a=rwkv_k_a,
             rwkv_r_k=rwkv_r_k, rwkv_ln_w=rwkv_ln_w, rwkv_ln_b=rwkv_ln_b, w_branch_a=w_branch_a,
             w_branch_b=w_branch_b, w_out=w_out, norm_mlp=norm_mlp, w_mlp_in=w_mlp_in,
             w_mlp_out=w_mlp_out)

    def prompt_attend(q, k, v, q_idx, w_idx, k_idx):
        return _prompt_attention(q, k, v, q_idx, w_idx, k_idx, rel_bias)

    def sample_attend(q, k, v, q_idx, w_idx, k_idx):
        return _sample_attention(q, k, v, q_idx, w_idx, k_idx, cache_k, cache_v, cache_idx_k,
                                 page_table, rel_bias)

    b_p = x_prompt.shape[0]
    zero_shift = jnp.zeros((b_p, D_MODEL), x_prompt.dtype)
    zero_wkv = jnp.zeros((b_p, N_HEADS_B, HEAD_B, HEAD_B), jnp.float32)

    h_p, k_prompt, v_prompt, idxk_prompt, wkv_prompt, shift_prompt = _layer(
        x_prompt, zero_shift, zero_wkv, prompt_attend, p)
    h_s, k_sample, v_sample, idxk_sample, wkv_sample, shift_sample = _layer(
        x_sample, state_shift, state_wkv, sample_attend, p)

    y_prompt = _rmsnorm(h_p, norm_final)
    y_sample = _rmsnorm(h_s, norm_final)
    return (y_prompt, y_sample, k_prompt, v_prompt, idxk_prompt, wkv_prompt, shift_prompt,
            k_sample, v_sample, idxk_sample, wkv_sample, shift_sample)
```

```python
import functools
import math

import numpy as np
import jax
import jax.numpy as jnp
from jax import lax
from jax.experimental import pallas as pl
from jax.experimental.pallas import tpu as pltpu

F32 = jnp.float32
BF16 = jnp.bfloat16
I32 = jnp.int32

D_MODEL = 1024
PAGE = 128
HEAD_DIM = 64
A_WIDTH = 512
N_HEADS_A = 8
IDX_HEADS = 4
IDX_DIM = 64
IDX_SCALE = (IDX_HEADS * IDX_DIM) ** -0.5
ATT_SCALE = HEAD_DIM ** -0.5
TOPK_MAX = 256
REL_BUCKETS = 32
REL_MAX_DIST = 128
HEAD_B = 64
B_WIDTH = 512
N_HEADS_B = 8
D_DECAY = 64
D_AAA = 64
D_GATE = 128
GN_EPS = 64e-5
D_FF = 4096
RMS_EPS = 1e-6
RWKV_COLS = 3 * B_WIDTH + D_DECAY + D_AAA + D_GATE
RWKV_OFF = 3 * A_WIDTH + IDX_HEADS * IDX_DIM + IDX_DIM + IDX_HEADS

LANE = 128
QB = 128
KC = 512
NEG_INF = float("-inf")
INT_MIN = -(2 ** 31)
VMEM_LIMIT = 56 * 1024 * 1024


def _cparams(n_axes):
    return pltpu.CompilerParams(dimension_semantics=("arbitrary",) * n_axes,
                                vmem_limit_bytes=VMEM_LIMIT)


def _resident(shape, index_map):
    return pl.BlockSpec(shape, index_map, pipeline_mode=pl.Buffered(1))


def _bucket_table(dist):
    dist = np.asarray(dist, np.int64)
    max_exact = REL_BUCKETS // 2
    d_f = np.maximum(dist, max_exact).astype(np.float32)
    large = max_exact + (np.log(d_f / np.float32(max_exact)) / np.float32(math.log(REL_MAX_DIST / max_exact))
                         * np.float32(REL_BUCKETS - max_exact)).astype(np.int32)
    large = np.minimum(large, REL_BUCKETS - 1)
    return np.where(dist < max_exact, dist, large).astype(np.int32)


def _window_buckets():
    t = np.arange(QB)[:, None]
    c = np.arange(2 * QB)[None, :]
    return _bucket_table(np.maximum(QB + t - c, 0))


def _rms(x, g):
    return x * lax.rsqrt(jnp.mean(x * x, axis=-1, keepdims=True) + RMS_EPS) * g


def _dot(a, b):
    return jnp.dot(a, b, preferred_element_type=F32)


def _dot_t(a, b):
    return lax.dot_general(a, b, (((1,), (1,)), ((), ())), preferred_element_type=F32)


def _split_dot(x, m_bf16):
    hi = x.astype(BF16)
    lo = (x - hi.astype(F32)).astype(BF16)
    return _dot(hi, m_bf16) + _dot(lo, m_bf16)


def _sort_key(s):
    u = pltpu.bitcast(s, I32)
    return jnp.where(u < 0, -(u & 0x7FFFFFFF), u)


def _proj_attn_kernel(x_ref, g_ref, w_ref, q_ref, k_ref, v_ref, kt_ref, vt_ref, qi_ref, tail_ref, kit_ref,
                      hl_ref, *, tm, hl_rows):
    h = _rms(x_ref[...], g_ref[...])
    hl_ref[...] = h[tm - hl_rows:, :]
    hb = h.astype(BF16)
    q = _dot(hb, w_ref[:, 0:512]) * ATT_SCALE
    for hd in range(N_HEADS_A):
        q_ref[hd] = q[:, hd * 64:(hd + 1) * 64].astype(BF16)
    k = _dot(hb, w_ref[:, 512:1024])
    k_ref[...] = k
    kt_ref[0] = k.T.astype(BF16)
    v = _dot(hb, w_ref[:, 1024:1536])
    v_ref[...] = v
    vt_ref[0] = v.T.astype(BF16)
    qi = _dot(hb, w_ref[:, 1536:1792])
    for hd in range(IDX_HEADS):
        qi_ref[hd] = qi[:, hd * 64:(hd + 1) * 64].astype(BF16)
    tail = _dot(hb, w_ref[:, 1792:1920])
    tail_ref[...] = tail
    kit_ref[0] = tail.T[0:64, :].astype(BF16)


def _proj_attn(x2d, g, w_attn, tm, hl_rows):
    n = x2d.shape[0]
    nt = n // tm
    row = lambda i: (i, 0)
    return pl.pallas_call(
        functools.partial(_proj_attn_kernel, tm=tm, hl_rows=hl_rows),
        grid=(nt,),
        in_specs=[pl.BlockSpec((tm, D_MODEL), row),
                  pl.BlockSpec((1, D_MODEL), lambda i: (0, 0)),
                  _resident((D_MODEL, 1920), lambda i: (0, 0))],
        out_specs=[pl.BlockSpec((N_HEADS_A, tm, 64), lambda i: (0, i, 0)),
                   pl.BlockSpec((tm, 512), row),
                   pl.BlockSpec((tm, 512), row),
                   pl.BlockSpec((1, 512, tm), lambda i: (i, 0, 0)),
                   pl.BlockSpec((1, 512, tm), lambda i: (i, 0, 0)),
                   pl.BlockSpec((IDX_HEADS, tm, 64), lambda i: (0, i, 0)),
                   pl.BlockSpec((tm, 128), row),
                   pl.BlockSpec((1, 64, tm), lambda i: (i, 0, 0)),
                   pl.BlockSpec((hl_rows, D_MODEL), row)],
        out_shape=[jax.ShapeDtypeStruct((N_HEADS_A, n, 64), BF16),
                   jax.ShapeDtypeStruct((n, 512), F32),
                   jax.ShapeDtypeStruct((n, 512), F32),
                   jax.ShapeDtypeStruct((nt, 512, tm), BF16),
                   jax.ShapeDtypeStruct((nt, 512, tm), BF16),
                   jax.ShapeDtypeStruct((IDX_HEADS, n, 64), BF16),
                   jax.ShapeDtypeStruct((n, 128), F32),
                   jax.ShapeDtypeStruct((nt, 64, tm), BF16),
                   jax.ShapeDtypeStruct((nt * hl_rows, D_MODEL), F32)],
        compiler_params=_cparams(1),
        name="proj_attn",
    )(x2d, g, w_attn)


def _proj_rest_kernel(x_ref, g_ref, w_ref, rw_ref, gate_ref):
    hb = _rms(x_ref[...], g_ref[...]).astype(BF16)
    rw_ref[...] = _dot(hb, w_ref[:, 0:RWKV_COLS])
    gate_ref[...] = _dot(hb, w_ref[:, RWKV_COLS:RWKV_COLS + 2 * D_MODEL])


def _proj_rest(x2d, g, w_rest, tm):
    n = x2d.shape[0]
    row = lambda i: (i, 0)
    return pl.pallas_call(
        _proj_rest_kernel,
        grid=(n // tm,),
        in_specs=[pl.BlockSpec((tm, D_MODEL), row),
                  pl.BlockSpec((1, D_MODEL), lambda i: (0, 0)),
                  _resident((D_MODEL, RWKV_COLS + 2 * D_MODEL), lambda i: (0, 0))],
        out_specs=[pl.BlockSpec((tm, RWKV_COLS), row),
                   pl.BlockSpec((tm, 2 * D_MODEL), row)],
        out_shape=[jax.ShapeDtypeStruct((n, RWKV_COLS), F32),
                   jax.ShapeDtypeStruct((n, 2 * D_MODEL), F32)],
        compiler_params=_cparams(1),
        name="proj_rest",
    )(x2d, g, w_rest)


def _shift_proj_kernel(s_ref, w_ref, o_ref):
    o_ref[...] = _dot(s_ref[...].astype(BF16), w_ref[...])


def _shift_proj(shift_rows, w_rw):
    b = shift_rows.shape[0]
    return pl.pallas_call(
        _shift_proj_kernel,
        out_shape=jax.ShapeDtypeStruct((b, RWKV_COLS), F32),
        compiler_params=pltpu.CompilerParams(vmem_limit_bytes=VMEM_LIMIT),
        name="shift_proj",
    )(shift_rows, w_rw)


def _bias_delta(btab, rb_ref, hd):
    out = jnp.zeros(btab.shape, F32)
    far = rb_ref[REL_BUCKETS - 1, hd]
    for j in range(REL_BUCKETS - 1):
        out = jnp.where(btab == j, rb_ref[j, hd] - far, out)
    return out


def _tri_ones():
    r = lax.broadcasted_iota(I32, (LANE, LANE), 0)
    c = lax.broadcasted_iota(I32, (LANE, LANE), 1)
    return jnp.where(r <= c, 1.0, 0.0).astype(BF16), jnp.ones((LANE, LANE), BF16)


def _prompt_attn_kernel(btab_ref, rb_ref, q_ref, qi_ref, tail_ref, kit_ref, kt_ref, vt_ref, o_ref,
                        keyv, msk, lg, dt, wbs, *, topk):
    b = pl.program_id(0)
    i = pl.program_id(1)
    jpc = KC // LANE
    nc = i // jpc + 1

    @pl.when((b == 0) & (i == 0))
    def _():
        bt = btab_ref[...]
        for hd in range(N_HEADS_A):
            dt[hd] = _bias_delta(bt, rb_ref, hd)

    row = lax.broadcasted_iota(I32, (QB, LANE), 0) + i * QB
    lane = lax.broadcasted_iota(I32, (QB, LANE), 1)

    w4 = tail_ref[:, 64:68] * IDX_SCALE
    for hd in range(IDX_HEADS):
        wbs[hd] = jnp.broadcast_to(w4[:, hd:hd + 1], (QB, LANE))
    q4 = qi_ref[...].reshape(IDX_HEADS * QB, IDX_DIM)

    def score_chunk(c, carry):
        d = _dot(q4, kit_ref[c])
        for j in range(jpc):
            s = jnp.zeros((QB, LANE), F32)
            for hd in range(IDX_HEADS):
                s = s + jnp.maximum(d[hd * QB:(hd + 1) * QB, j * LANE:(j + 1) * LANE], 0.0) * wbs[hd]
            kp = c * KC + j * LANE + lane
            keyv[c * jpc + j] = _sort_key(jnp.where(kp <= row, s, NEG_INF))
        return carry

    lax.fori_loop(0, nc, score_chunk, 0)

    def count_ge(cand):
        def body(c, cnt):
            for j in range(jpc):
                cnt = cnt + jnp.where(keyv[c * jpc + j] >= cand, 1.0, 0.0)
            return cnt
        cnt = lax.fori_loop(0, nc, body, jnp.zeros((QB, LANE), F32))
        return jnp.sum(cnt, axis=-1, keepdims=True)

    def bit_step(it, t_u):
        cand_u = t_u | jnp.left_shift(jnp.int32(1), 31 - it)
        n_ge = count_ge(cand_u ^ INT_MIN)
        return jnp.where(n_ge >= float(topk), cand_u, t_u)

    t_u = lax.fori_loop(0, 32, bit_step, jnp.zeros((QB, LANE), I32))
    thr = t_u ^ INT_MIN

    def count_gt(c, cnt):
        for j in range(jpc):
            cnt = cnt + jnp.where(keyv[c * jpc + j] > thr, 1.0, 0.0)
        return cnt
    n_gt = jnp.sum(lax.fori_loop(0, nc, count_gt, jnp.zeros((QB, LANE), F32)), axis=-1, keepdims=True)
    need = float(topk) - n_gt
    tri, ones = _tri_ones()

    def mask_chunk(c, off):
        for j in range(jpc):
            kv = keyv[c * jpc + j]
            eq = kv == thr
            eqb = jnp.where(eq, 1.0, 0.0).astype(BF16)
            pre = off + _dot(eqb, tri)
            kp = c * KC + j * LANE + lane
            sel = ((kv > thr) | (eq & (pre <= need))) & (kp <= row)
            msk[c * jpc + j] = jnp.where(sel, 0.0, NEG_INF)
            off = off + _dot(eqb, ones)
        return off

    lax.fori_loop(0, nc, mask_chunk, jnp.zeros((QB, LANE), F32))

    for hd in range(N_HEADS_A):
        qh = q_ref[hd]
        hs = slice(hd * HEAD_DIM, (hd + 1) * HEAD_DIM)

        def logit_chunk(c, m, near):
            lgt = _dot(qh, kt_ref[c, hs, :])
            for j in range(jpc):
                blk = c * jpc + j
                x = lgt[:, j * LANE:(j + 1) * LANE] + msk[blk]
                if near:
                    x = x + jnp.where(blk == i, dt[hd, :, LANE:], jnp.where(blk == i - 1, dt[hd, :, :LANE], 0.0))
                lg[blk] = x
                m = jnp.maximum(m, x)
            return m

        n_far = jnp.maximum(nc - 2, 0)
        m = lax.fori_loop(0, n_far, functools.partial(logit_chunk, near=False),
                          jnp.full((QB, LANE), NEG_INF, F32))
        m = lax.fori_loop(n_far, nc, functools.partial(logit_chunk, near=True), m)
        mrow = jnp.max(m, axis=-1, keepdims=True)

        def pv_chunk(c, carry):
            l, acc = carry
            ps = []
            for j in range(jpc):
                p = jnp.exp(lg[c * jpc + j] - mrow)
                l = l + p
                ps.append(p.astype(BF16))
            acc = acc + _dot_t(jnp.concatenate(ps, axis=1), vt_ref[c, hs, :])
            return l, acc

        l, acc = lax.fori_loop(0, nc, pv_chunk, (jnp.zeros((QB, LANE), F32), jnp.zeros((QB, HEAD_DIM), F32)))
        o_ref[:, hs] = acc / jnp.sum(l, axis=-1, keepdims=True)


def _prompt_attn(btab, rel_bias, q_hm, qi_hm, tail, kit, kt, vt, batch, seq):
    nq = seq // QB
    ncb = seq // KC
    nkb = seq // LANE
    topk = min(TOPK_MAX, seq // 4)
    return pl.pallas_call(
        functools.partial(_prompt_attn_kernel, topk=topk),
        grid=(batch, nq),
        in_specs=[pl.BlockSpec((QB, 2 * QB), lambda b, i: (0, 0)),
                  pl.BlockSpec(memory_space=pltpu.SMEM),
                  pl.BlockSpec((N_HEADS_A, QB, 64), lambda b, i: (0, b * nq + i, 0)),
                  pl.BlockSpec((IDX_HEADS, QB, 64), lambda b, i: (0, b * nq + i, 0)),
                  pl.BlockSpec((QB, 128), lambda b, i: (b * nq + i, 0)),
                  _resident((ncb, 64, KC), lambda b, i: (b, 0, 0)),
                  _resident((ncb, 512, KC), lambda b, i: (b, 0, 0)),
                  _resident((ncb, 512, KC), lambda b, i: (b, 0, 0))],
        out_specs=pl.BlockSpec((QB, A_WIDTH), lambda b, i: (b * nq + i, 0)),
        out_shape=jax.ShapeDtypeStruct((batch * seq, A_WIDTH), F32),
        scratch_shapes=[pltpu.VMEM((nkb, QB, LANE), I32),
                        pltpu.VMEM((nkb, QB, LANE), F32),
                        pltpu.VMEM((nkb, QB, LANE), F32),
                        pltpu.VMEM((N_HEADS_A, QB, 2 * QB), F32),
                        pltpu.VMEM((IDX_HEADS, QB, LANE), F32)],
        compiler_params=_cparams(2),
        name="prompt_attn",
    )(btab, rel_bias, q_hm, qi_hm, tail, kit, kt, vt)


SEL_PAGES = 16
ATT_PAGES = 8


def _sample_select_kernel(pt_ref, qi_ref, tail_ref, *rest, n_groups, t_new, topk):
    pages = rest[:SEL_PAGES]
    mpast_ref, mnew_ref, keyv, keyn = rest[SEL_PAGES:]
    g = pl.program_id(1)
    gw = SEL_PAGES * PAGE
    w4 = tail_ref[:, 64:68] * IDX_SCALE
    q4 = qi_ref[...].reshape(IDX_HEADS * t_new, IDX_DIM)

    def scores(d):
        s = jnp.zeros((t_new, d.shape[1]), F32)
        for hd in range(IDX_HEADS):
            s = s + jnp.maximum(d[hd * t_new:(hd + 1) * t_new, :], 0.0) * w4[:, hd:hd + 1]
        return s

    kpast = jnp.concatenate([p[0] for p in pages], axis=0).astype(BF16)
    keyv[g] = _sort_key(scores(_dot_t(q4, kpast)))

    @pl.when(g == n_groups - 1)
    def _():
        rown = lax.broadcasted_iota(I32, (t_new, LANE), 0)
        lanen = lax.broadcasted_iota(I32, (t_new, LANE), 1)
        knew = jnp.concatenate([tail_ref[:, 0:64], jnp.zeros((LANE - t_new, 64), F32)], axis=0).astype(BF16)
        sn = scores(_dot_t(q4, knew))
        causal_n = lanen <= rown
        keyn[...] = _sort_key(jnp.where(causal_n, sn, NEG_INF))

        def count(cmp):
            cnt = jnp.zeros((t_new, LANE), F32)
            for gg in range(n_groups):
                hit = jnp.where(cmp(keyv[gg]), 1.0, 0.0)
                for j in range(gw // LANE):
                    cnt = cnt + hit[:, j * LANE:(j + 1) * LANE]
            cnt = cnt + jnp.where(cmp(keyn[...]), 1.0, 0.0)
            return jnp.sum(cnt, axis=-1, keepdims=True)

        def bit_step(it, t_u):
            cand_u = t_u | jnp.left_shift(jnp.int32(1), 31 - it)
            cand = cand_u ^ INT_MIN
            n_ge = count(lambda kv: kv >= cand)
            return jnp.where(n_ge >= float(topk), cand_u, t_u)

        t_u = lax.fori_loop(0, 32, bit_step, jnp.zeros((t_new, 1), I32))
        thr = t_u ^ INT_MIN
        need = float(topk) - count(lambda kv: kv > thr)
        tri, ones = _tri_ones()
        off = jnp.zeros((t_new, LANE), F32)
        for gg in range(n_groups):
            kvg = keyv[gg]
            for j in range(gw // LANE):
                kv = kvg[:, j * LANE:(j + 1) * LANE]
                eq = kv == thr
                eqb = jnp.where(eq, 1.0, 0.0).astype(BF16)
                sel = (kv > thr) | (eq & (off + _dot(eqb, tri) <= need))
                mpast_ref[0, :, gg * gw + j * LANE:gg * gw + (j + 1) * LANE] = jnp.where(sel, 0.0, NEG_INF)
                off = off + _dot(eqb, ones)
        kv = keyn[...]
        eq = kv == thr
        eqb = jnp.where(eq, 1.0, 0.0).astype(BF16)
        sel = ((kv > thr) | (eq & (off + _dot(eqb, tri) <= need))) & causal_n
        mnew_ref[0] = jnp.where(sel, 0.0, NEG_INF)


def _sample_select(page_table, qi_hm, tail, cache_idx_k, t_new):
    bsz, n_pages = page_table.shape
    n_groups = n_pages // SEL_PAGES
    past = n_pages * PAGE
    topk = min(TOPK_MAX, (past + t_new) // 4)

    def page_spec(j):
        return pl.BlockSpec((1, PAGE, IDX_DIM), lambda b, g, pt: (pt[b, g * SEL_PAGES + j], 0, 0))

    grid_spec = pltpu.PrefetchScalarGridSpec(
        num_scalar_prefetch=1,
        grid=(bsz, n_groups),
        in_specs=[pl.BlockSpec((IDX_HEADS, t_new, 64), lambda b, g, pt: (0, b, 0)),
                  pl.BlockSpec((t_new, 128), lambda b, g, pt: (b, 0))]
                 + [page_spec(j) for j in range(SEL_PAGES)],
        out_specs=[pl.BlockSpec((1, t_new, past), lambda b, g, pt: (b, 0, 0)),
                   pl.BlockSpec((1, t_new, LANE), lambda b, g, pt: (b, 0, 0))],
        scratch_shapes=[pltpu.VMEM((n_groups, t_new, SEL_PAGES * PAGE), I32),
                        pltpu.VMEM((t_new, LANE), I32)])
    return pl.pallas_call(
        functools.partial(_sample_select_kernel, n_groups=n_groups, t_new=t_new, topk=topk),
        grid_spec=grid_spec,
        out_shape=[jax.ShapeDtypeStruct((bsz, t_new, past), F32),
                   jax.ShapeDtypeStruct((bsz, t_new, LANE), F32)],
        compiler_params=_cparams(2),
        name="sample_select",
    )(page_table, qi_hm, tail, *([cache_idx_k] * SEL_PAGES))


def _sample_attn_kernel(pt_ref, btab_ref, rb_ref, q_ref, knew_ref, vnew_ref, mpast_ref, mnew_ref, *rest,
                        n_steps, t_new):
    kpages = rest[:ATT_PAGES]
    vpages = rest[ATT_PAGES:2 * ATT_PAGES]
    o_ref, qbd, dq, m_s, l_s, acc_s = rest[2 * ATT_PAGES:]
    g = pl.program_id(1)
    hq = N_HEADS_A * t_new
    gw = ATT_PAGES * PAGE
    head_of_row = lax.broadcasted_iota(I32, (hq, A_WIDTH), 0) // t_new
    head_of_col = lax.broadcasted_iota(I32, (hq, A_WIDTH), 1) // HEAD_DIM
    own = head_of_row == head_of_col

    @pl.when((pl.program_id(0) == 0) & (g == 0))
    def _():
        bt = btab_ref[0:t_new, :]
        for hd in range(N_HEADS_A):
            dq[hd * t_new:(hd + 1) * t_new, :] = _bias_delta(bt, rb_ref, hd)

    @pl.when(g == 0)
    def _():
        qt = jnp.concatenate([q_ref[...].astype(F32)] * N_HEADS_A, axis=0)
        qbd[...] = jnp.where(own, qt, 0.0).astype(BF16)
        m_s[...] = jnp.full((hq, 1), NEG_INF, F32)
        l_s[...] = jnp.zeros((hq, 1), F32)
        acc_s[...] = jnp.zeros((hq, A_WIDTH), F32)

    def online(x, vb):
        m_old = m_s[...]
        m_new = jnp.maximum(m_old, jnp.max(x, axis=-1, keepdims=True))
        m_fin = jnp.where(m_new == NEG_INF, 0.0, m_new)
        alpha = jnp.exp(m_old - m_fin)
        p = jnp.exp(x - m_fin)
        l_s[...] = alpha * l_s[...] + jnp.sum(p, axis=-1, keepdims=True)
        acc_s[...] = alpha * acc_s[...] + _dot(p.astype(BF16), vb)
        m_s[...] = m_new

    kb = jnp.concatenate([p[0] for p in kpages], axis=0).astype(BF16)
    vb = jnp.concatenate([p[0] for p in vpages], axis=0).astype(BF16)
    x = _dot_t(qbd[...], kb) + jnp.concatenate([mpast_ref[0]] * N_HEADS_A, axis=0)
    lane = lax.broadcasted_iota(I32, (hq, gw), 1)
    near = jnp.concatenate([jnp.zeros((hq, gw - LANE), F32), dq[:, :LANE]], axis=1)
    x = x + jnp.where(g == n_steps - 1, near, 0.0)

    online(x, vb)

    @pl.when(g == n_steps - 1)
    def _():
        pad = jnp.zeros((LANE - t_new, A_WIDTH), F32)
        kn = jnp.concatenate([knew_ref[...], pad], axis=0).astype(BF16)
        vn = jnp.concatenate([vnew_ref[...], pad], axis=0).astype(BF16)
        xn = _dot_t(qbd[...], kn) + jnp.concatenate([mnew_ref[0]] * N_HEADS_A, axis=0) + dq[:, LANE:]
        online(xn, vn)
        res = jnp.where(own, acc_s[...] / l_s[...], 0.0)
        out = res[0:t_new, :]
        for hd in range(1, N_HEADS_A):
            out = out + res[hd * t_new:(hd + 1) * t_new, :]
        o_ref[...] = out


def _sample_attn(page_table, btab, rel_bias, q_rows, k_new, v_new, mpast, mnew, cache_k, cache_v, t_new):
    bsz, n_pages = page_table.shape
    n_steps = n_pages // ATT_PAGES
    hq = N_HEADS_A * t_new

    def page_spec(j):
        return pl.BlockSpec((1, PAGE, A_WIDTH), lambda b, g, pt: (pt[b, g * ATT_PAGES + j], 0, 0))

    grid_spec = pltpu.PrefetchScalarGridSpec(
        num_scalar_prefetch=1,
        grid=(bsz, n_steps),
        in_specs=[pl.BlockSpec((QB, 2 * QB), lambda b, g, pt: (0, 0)),
                  pl.BlockSpec(memory_space=pltpu.SMEM),
                  pl.BlockSpec((t_new, A_WIDTH), lambda b, g, pt: (b, 0)),
                  pl.BlockSpec((t_new, A_WIDTH), lambda b, g, pt: (b, 0)),
                  pl.BlockSpec((t_new, A_WIDTH), lambda b, g, pt: (b, 0)),
                  pl.BlockSpec((1, t_new, ATT_PAGES * PAGE), lambda b, g, pt: (b, 0, g)),
                  pl.BlockSpec((1, t_new, LANE), lambda b, g, pt: (b, 0, 0))]
                 + [page_spec(j) for j in range(ATT_PAGES)] * 2,
        out_specs=pl.BlockSpec((t_new, A_WIDTH), lambda b, g, pt: (b, 0)),
        scratch_shapes=[pltpu.VMEM((hq, A_WIDTH), BF16),
                        pltpu.VMEM((hq, 2 * QB), F32),
                        pltpu.VMEM((hq, 1), F32),
                        pltpu.VMEM((hq, 1), F32),
                        pltpu.VMEM((hq, A_WIDTH), F32)])
    return pl.pallas_call(
        functools.partial(_sample_attn_kernel, n_steps=n_steps, t_new=t_new),
        grid_spec=grid_spec,
        out_shape=jax.ShapeDtypeStruct((bsz * t_new, A_WIDTH), F32),
        compiler_params=_cparams(2),
        name="sample_attn",
    )(page_table, btab, rel_bias, q_rows, k_new, v_new, mpast, mnew,
      *([cache_k] * ATT_PAGES), *([cache_v] * ATT_PAGES))


def _head_block_ones():
    r = lax.broadcasted_iota(I32, (B_WIDTH, B_WIDTH), 0) // HEAD_B
    c = lax.broadcasted_iota(I32, (B_WIDTH, B_WIDTH), 1) // HEAD_B
    return jnp.where(r == c, 1.0, 0.0).astype(BF16)


def _rwkv_prep_kernel(rw_ref, prev8_ref, prow_ref, mu_ref, w0_ref, a0_ref, kk_ref, ka_ref, rk_ref,
                      w2_ref, a2_ref, g2_ref,
                      r_ref, w_ref, k_ref, v_ref, na_ref, bb_ref, bonus_ref, g_ref, *, tt):
    t = pl.program_id(1)
    cur = rw_ref[0]
    first_prev = jnp.where(t == 0, prow_ref[0], prev8_ref[0, 7:8, :])
    rows = lax.broadcasted_iota(I32, cur.shape, 0)
    prev = jnp.where(rows == 0, first_prev, pltpu.roll(cur, 1, 0))
    mixed = cur + (prev - cur) * mu_ref[...]
    r = mixed[:, 0:512]
    k = mixed[:, 512:1024]
    v = mixed[:, 1024:1536]
    wd = mixed[:, 1536:1600]
    ad = mixed[:, 1600:1664]
    gd = mixed[:, 1664:1792]
    z = w0_ref[...] + _dot(jnp.tanh(wd).astype(BF16), w2_ref[...])
    u = -z
    softplus = jnp.maximum(u, 0.0) + jnp.log1p(jnp.exp(-jnp.abs(u)))
    decay = jnp.exp(-jnp.exp(-softplus - 0.5))
    a = jax.nn.sigmoid(a0_ref[...] + _dot(ad.astype(BF16), a2_ref[...]))
    g = _dot(jax.nn.sigmoid(gd).astype(BF16), g2_ref[...])
    bd = _head_block_ones()
    kk = k * kk_ref[...]
    nrm = jnp.sqrt(_split_dot(kk * kk, bd))
    kk = kk / jnp.maximum(nrm, 1e-12)
    k2 = k * (1.0 + (a - 1.0) * ka_ref[...])
    r_ref[0] = r
    w_ref[0] = decay
    k_ref[0] = k2
    v_ref[0] = v
    na_ref[0] = -kk
    bb_ref[0] = kk * a
    bonus_ref[0] = _split_dot(r * k2 * rk_ref[...], bd) * v
    g_ref[0] = g


def _rwkv_prep(rw3, prow, p, tt):
    bsz, t_len, _ = rw3.shape
    vec = lambda c: pl.BlockSpec((1, c), lambda b, t: (0, 0))
    full = lambda r, c: pl.BlockSpec((r, c), lambda b, t: (0, 0))
    tile = pl.BlockSpec((1, tt, B_WIDTH), lambda b, t: (b, t, 0))
    return pl.pallas_call(
        functools.partial(_rwkv_prep_kernel, tt=tt),
        grid=(bsz, t_len // tt),
        in_specs=[pl.BlockSpec((1, tt, RWKV_COLS), lambda b, t: (b, t, 0)),
                  pl.BlockSpec((1, 8, RWKV_COLS), lambda b, t: (b, jnp.maximum(t * (tt // 8) - 1, 0), 0)),
                  pl.BlockSpec((1, 1, RWKV_COLS), lambda b, t: (b, 0, 0)),
                  vec(RWKV_COLS), vec(B_WIDTH), vec(B_WIDTH), vec(B_WIDTH), vec(B_WIDTH), vec(B_WIDTH),
                  full(D_DECAY, B_WIDTH), full(D_AAA, B_WIDTH), full(D_GATE, B_WIDTH)],
        out_specs=[tile] * 8,
        out_shape=[jax.ShapeDtypeStruct((bsz, t_len, B_WIDTH), F32)] * 8,
        compiler_params=_cparams(2),
        name="rwkv_prep",
    )(rw3, rw3, prow.reshape(bsz, 1, RWKV_COLS), p["mu"], p["w0"], p["a0"], p["k_k"], p["k_a"], p["r_k"],
      p["w2"], p["a2"], p["g2"])


def _rwkv_scan_kernel(r_ref, w_ref, k_ref, v_ref, na_ref, bb_ref, s0_ref, y_ref, st_ref, *, bb_n, tc):
    @pl.when(pl.program_id(1) == 0)
    def _():
        st_ref[...] = s0_ref[...]

    lane = lax.broadcasted_iota(I32, (HEAD_B, LANE), 1)
    rowi = lax.broadcasted_iota(I32, (HEAD_B, LANE), 0)
    lo = lane < HEAD_B
    diag_lo = lane == rowi
    diag_hi = lane == rowi + HEAD_B
    diag = diag_lo | diag_hi
    lo1 = lo[0:1]

    def halves(x):
        return jnp.where(lo1, x, 0.0), jnp.where(lo1, 0.0, x)

    def seg_sum(x, x_lo, x_hi):
        s0 = jnp.sum(x * x_lo, axis=-1, keepdims=True)
        s1 = jnp.sum(x * x_hi, axis=-1, keepdims=True)
        return jnp.where(lo, s0, s1)

    def tile(t8, carry):
        t0 = pl.multiple_of(t8 * 8, 8)
        for bi in range(bb_n):
            for pr in range(N_HEADS_B // 2):
                ls = slice(pr * LANE, (pr + 1) * LANE)
                rr, ww, kk, vv, aa, bb = (ref[bi, pl.ds(t0, 8), ls]
                                          for ref in (r_ref, w_ref, k_ref, v_ref, na_ref, bb_ref))
                s = st_ref[bi, pr]
                yrows = []
                for i in range(8):
                    row = lambda x: x[i:i + 1, :]
                    sab = seg_sum(s, *halves(row(aa)))
                    v_i = row(vv)
                    vc = jnp.where(lo, jnp.sum(jnp.where(diag_lo, v_i, 0.0), axis=-1, keepdims=True),
                                   jnp.sum(jnp.where(diag_hi, v_i, 0.0), axis=-1, keepdims=True))
                    s = s * row(ww) + sab * row(bb) + vc * row(kk)
                    yb = seg_sum(s, *halves(row(rr)))
                    yrows.append(jnp.sum(jnp.where(diag, yb, 0.0), axis=0, keepdims=True))
                st_ref[bi, pr] = s
                y_ref[bi, pl.ds(t0, 8), ls] = jnp.concatenate(yrows, axis=0)
        return carry

    lax.fori_loop(0, tc // 8, tile, 0)


def _pair_pack(state):
    b = state.shape[0]
    s = state.reshape(b, N_HEADS_B // 2, 2, HEAD_B, HEAD_B)
    return jnp.transpose(s, (0, 1, 3, 2, 4)).reshape(b, N_HEADS_B // 2, HEAD_B, 2 * HEAD_B)


def _pair_unpack(state):
    b = state.shape[0]
    s = state.reshape(b, N_HEADS_B // 2, HEAD_B, 2, HEAD_B)
    return jnp.transpose(s, (0, 1, 3, 2, 4)).reshape(b, N_HEADS_B, HEAD_B, HEAD_B)


def _rwkv_scan(r, w, k, v, na, bb, state0, bb_n, tc):
    bsz, t_len, _ = r.shape
    tile = pl.BlockSpec((bb_n, tc, B_WIDTH), lambda b, t: (b, t, 0))
    st = pl.BlockSpec((bb_n, N_HEADS_B // 2, HEAD_B, 2 * HEAD_B), lambda b, t: (b, 0, 0, 0))
    y, state = pl.pallas_call(
        functools.partial(_rwkv_scan_kernel, bb_n=bb_n, tc=tc),
        grid=(bsz // bb_n, t_len // tc),
        in_specs=[tile] * 6 + [st],
        out_specs=[tile, st],
        out_shape=[jax.ShapeDtypeStruct((bsz, t_len, B_WIDTH), F32),
                   jax.ShapeDtypeStruct((bsz, N_HEADS_B // 2, HEAD_B, 2 * HEAD_B), F32)],
        compiler_params=_cparams(2),
        name="rwkv_scan",
    )(r, w, k, v, na, bb, _pair_pack(state0))
    return y, _pair_unpack(state)


def _tail_kernel(x_ref, att_ref, y_ref, bonus_ref, g_ref, gate_ref, lnw_ref, lnb_ref, nm_ref, nf_ref,
                 wa_ref, wb_ref, wo_ref, w1_ref, w2_ref, o_ref):
    bd = _head_block_ones()
    y = y_ref[...]
    mean = _split_dot(y, bd) * (1.0 / HEAD_B)
    yc = y - mean
    var = _split_dot(yc * yc, bd) * (1.0 / HEAD_B)
    yn = yc * lax.rsqrt(var + GN_EPS) * lnw_ref[...] + lnb_ref[...]
    rwkv = (yn + bonus_ref[...]) * g_ref[...]
    ga = gate_ref[:, 0:D_MODEL]
    gb = gate_ref[:, D_MODEL:2 * D_MODEL]
    merged = (jax.nn.sigmoid(ga) * _dot(att_ref[...].astype(BF16), wa_ref[...])
              + jax.nn.sigmoid(gb) * _dot(rwkv.astype(BF16), wb_ref[...]))
    x1 = x_ref[...] + _dot(merged.astype(BF16), wo_ref[...])
    hm = _rms(x1, nm_ref[...]).astype(BF16)
    up = jnp.maximum(_dot(hm, w1_ref[...]), 0.0)
    x2 = x1 + _dot((up * up).astype(BF16), w2_ref[...])
    o_ref[...] = _rms(x2, nf_ref[...])


def _tail(x2d, att, y, bonus, g, gates, p, tm):
    n = x2d.shape[0]
    row = lambda i: (i, 0)
    vec = lambda c: pl.BlockSpec((1, c), lambda i: (0, 0))
    res = lambda r, c: _resident((r, c), lambda i: (0, 0))
    return pl.pallas_call(
        _tail_kernel,
        grid=(n // tm,),
        in_specs=[pl.BlockSpec((tm, D_MODEL), row),
                  pl.BlockSpec((tm, A_WIDTH), row), pl.BlockSpec((tm, B_WIDTH), row),
                  pl.BlockSpec((tm, B_WIDTH), row), pl.BlockSpec((tm, B_WIDTH), row),
                  pl.BlockSpec((tm, 2 * D_MODEL), row),
                  vec(B_WIDTH), vec(B_WIDTH), vec(D_MODEL), vec(D_MODEL),
                  res(A_WIDTH, D_MODEL), res(B_WIDTH, D_MODEL), res(D_MODEL, D_MODEL),
                  res(D_MODEL, D_FF), res(D_FF, D_MODEL)],
        out_specs=pl.BlockSpec((tm, D_MODEL), row),
        out_shape=jax.ShapeDtypeStruct((n, D_MODEL), F32),
        compiler_params=_cparams(1),
        name="tail",
    )(x2d, att, y, bonus, g, gates, p["ln_w"], p["ln_b"], p["norm_mlp"], p["norm_final"],
      p["wa"], p["wb"], p["wo"], p["w1"], p["w2m"])


def _prepare_params(norm_mix, w_in, rwkv_mu, rwkv_w0, rwkv_w2, rwkv_a0, rwkv_a2, rwkv_g2, rwkv_k_k, rwkv_k_a,
                    rwkv_r_k, rwkv_ln_w, rwkv_ln_b, w_branch_a, w_branch_b, w_out, norm_mlp, w_mlp_in,
                    w_mlp_out, norm_final):
    wb16 = w_in.astype(BF16)
    o_idx = 3 * A_WIDTH
    o_kidx = o_idx + IDX_HEADS * IDX_DIM
    o_widx = o_kidx + IDX_DIM
    pad = jnp.zeros((D_MODEL, 128 - IDX_DIM - IDX_HEADS), BF16)
    w_attn = jnp.concatenate([wb16[:, :o_kidx], wb16[:, o_kidx:o_widx + IDX_HEADS], pad], axis=1)
    o = RWKV_OFF
    def regroup(a):
        return jnp.concatenate([a[..., 0:512], a[..., 576:1600], a[..., 512:576], a[..., 1600:1792]], axis=-1)
    w_rw = regroup(wb16[:, o:o + RWKV_COLS])
    w_rest = jnp.concatenate([w_rw, wb16[:, o + RWKV_COLS:]], axis=1)
    r1 = lambda a: a.reshape(1, -1).astype(F32)
    return dict(
        norm_mix=r1(norm_mix), w_attn=w_attn, w_rest=w_rest, w_rw=w_rw,
        mu=r1(regroup(rwkv_mu)), w0=r1(rwkv_w0), a0=r1(rwkv_a0), k_k=r1(rwkv_k_k), k_a=r1(rwkv_k_a),
        r_k=r1(rwkv_r_k), w2=rwkv_w2.astype(BF16), a2=rwkv_a2.astype(BF16), g2=rwkv_g2.astype(BF16),
        ln_w=r1(rwkv_ln_w), ln_b=r1(rwkv_ln_b), norm_mlp=r1(norm_mlp), norm_final=r1(norm_final),
        wa=w_branch_a.astype(BF16), wb=w_branch_b.astype(BF16), wo=w_out.astype(BF16),
        w1=w_mlp_in.astype(BF16), w2m=w_mlp_out.astype(BF16))


def _layer(x, shift_rows, wkv0, attend, p, tm, tm_tail, scan_tc):
    bsz, t_len, _ = x.shape
    n = bsz * t_len
    x2d = x.reshape(n, D_MODEL)
    hl_rows = 8 if t_len >= tm else tm
    q_hm, k, v, kt, vt, qi_hm, tail, kit, hlast = _proj_attn(x2d, p["norm_mix"], p["w_attn"], tm, hl_rows)
    rw, gates = _proj_rest(x2d, p["norm_mix"], p["w_rest"], tm)
    att = attend(q_hm, k, v, kt, vt, qi_hm, tail, kit)
    prow = _shift_proj(shift_rows, p["w_rw"])
    r, w, k2, v2, na, bb, bonus, g = _rwkv_prep(rw.reshape(bsz, t_len, RWKV_COLS), prow, p, min(256, t_len))
    y, wkv = _rwkv_scan(r, w, k2, v2, na, bb, wkv0, 2, scan_tc)
    flat = lambda a: a.reshape(n, B_WIDTH)
    out = _tail(x2d, att, flat(y), flat(bonus), flat(g), gates, p, tm_tail)
    if t_len >= tm:
        shift = hlast.reshape(bsz, t_len // tm, 8, D_MODEL)[:, -1, 7]
    else:
        shift = hlast.reshape(bsz, t_len, D_MODEL)[:, -1]
    return (out.reshape(bsz, t_len, D_MODEL), k.reshape(bsz, t_len, N_HEADS_A, HEAD_DIM),
            v.reshape(bsz, t_len, N_HEADS_A, HEAD_DIM), tail[:, :IDX_DIM].reshape(bsz, t_len, IDX_DIM), wkv, shift)


def kernel(x_prompt, x_sample, cache_k, cache_v, cache_idx_k, state_wkv, state_shift, page_table, rel_bias, norm_mix, w_in, rwkv_mu, rwkv_w0, rwkv_w2, rwkv_a0, rwkv_a2, rwkv_g2, rwkv_k_k, rwkv_k_a, rwkv_r_k, rwkv_ln_w, rwkv_ln_b, w_branch_a, w_branch_b, w_out, norm_mlp, w_mlp_in, w_mlp_out, norm_final):
    p = _prepare_params(norm_mix, w_in, rwkv_mu, rwkv_w0, rwkv_w2, rwkv_a0, rwkv_a2, rwkv_g2, rwkv_k_k,
                        rwkv_k_a, rwkv_r_k, rwkv_ln_w, rwkv_ln_b, w_branch_a, w_branch_b, w_out, norm_mlp,
                        w_mlp_in, w_mlp_out, norm_final)
    btab = jnp.asarray(_window_buckets())
    b_p, s_p, _ = x_prompt.shape
    b_s, t_s, _ = x_sample.shape
    n_pool = cache_k.shape[0]

    def prompt_attend(q_hm, k, v, kt, vt, qi_hm, tail, kit):
        return _prompt_attn(btab, rel_bias, q_hm, qi_hm, tail, kit, kt, vt, b_p, s_p)

    def sample_attend(q_hm, k, v, kt, vt, qi_hm, tail, kit):
        mpast, mnew = _sample_select(page_table, qi_hm, tail, cache_idx_k, t_s)
        q_rows = jnp.transpose(q_hm, (1, 0, 2)).reshape(b_s * t_s, A_WIDTH)
        return _sample_attn(page_table, btab, rel_bias, q_rows, k, v, mpast, mnew,
                            cache_k.reshape(n_pool, PAGE, A_WIDTH), cache_v.reshape(n_pool, PAGE, A_WIDTH), t_s)

    zero_shift = jnp.zeros((b_p, D_MODEL), F32)
    zero_wkv = jnp.zeros((b_p, N_HEADS_B, HEAD_B, HEAD_B), F32)
    y_p, k_p, v_p, ik_p, wkv_p, sh_p = _layer(x_prompt, zero_shift, zero_wkv, prompt_attend, p, KC, 256, 256)
    y_s, k_s, v_s, ik_s, wkv_s, sh_s = _layer(x_sample, state_shift, state_wkv, sample_attend, p,
                                              b_s * t_s, b_s * t_s, t_s)
    return (y_p, y_s, k_p, v_p, ik_p, wkv_p, sh_p, k_s, v_s, ik_s, wkv_s, sh_s)
```

```python
import functools
import math

import numpy as np
import jax
import jax.numpy as jnp
from jax import lax
from jax.experimental import pallas as pl
from jax.experimental.pallas import tpu as pltpu

F32 = jnp.float32
BF16 = jnp.bfloat16
I32 = jnp.int32

D_MODEL = 1024
PAGE = 128
HEAD_DIM = 64
A_WIDTH = 512
N_HEADS_A = 8
IDX_HEADS = 4
IDX_DIM = 64
IDX_SCALE = (IDX_HEADS * IDX_DIM) ** -0.5
ATT_SCALE = HEAD_DIM ** -0.5
TOPK_MAX = 256
REL_BUCKETS = 32
REL_MAX_DIST = 128
HEAD_B = 64
B_WIDTH = 512
N_HEADS_B = 8
D_DECAY = 64
D_AAA = 64
D_GATE = 128
GN_EPS = 64e-5
D_FF = 4096
RMS_EPS = 1e-6
RWKV_COLS = 3 * B_WIDTH + D_DECAY + D_AAA + D_GATE
RWKV_OFF = 3 * A_WIDTH + IDX_HEADS * IDX_DIM + IDX_DIM + IDX_HEADS

LANE = 128
QB = 128
KC = 512
NEG_INF = float("-inf")
INT_MIN = -(2 ** 31)
VMEM_LIMIT = 56 * 1024 * 1024


def _cparams(n_axes):
    return pltpu.CompilerParams(dimension_semantics=("arbitrary",) * n_axes,
                                vmem_limit_bytes=VMEM_LIMIT)


def _resident(shape, index_map):
    return pl.BlockSpec(shape, index_map, pipeline_mode=pl.Buffered(1))


def _bucket_table(dist):
    dist = np.asarray(dist, np.int64)
    max_exact = REL_BUCKETS // 2
    d_f = np.maximum(dist, max_exact).astype(np.float32)
    large = max_exact + (np.log(d_f / np.float32(max_exact)) / np.float32(math.log(REL_MAX_DIST / max_exact))
                         * np.float32(REL_BUCKETS - max_exact)).astype(np.int32)
    large = np.minimum(large, REL_BUCKETS - 1)
    return np.where(dist < max_exact, dist, large).astype(np.int32)


def _window_buckets():
    t = np.arange(QB)[:, None]
    c = np.arange(2 * QB)[None, :]
    return _bucket_table(np.maximum(QB + t - c, 0))


def _rms(x, g):
    return x * lax.rsqrt(jnp.mean(x * x, axis=-1, keepdims=True) + RMS_EPS) * g


def _dot(a, b):
    return jnp.dot(a, b, preferred_element_type=F32)


def _dot_t(a, b):
    return lax.dot_general(a, b, (((1,), (1,)), ((), ())), preferred_element_type=F32)


def _split_dot(x, m_bf16):
    hi = x.astype(BF16)
    lo = (x - hi.astype(F32)).astype(BF16)
    return _dot(hi, m_bf16) + _dot(lo, m_bf16)


def _sort_key(s):
    u = pltpu.bitcast(s, I32)
    return jnp.where(u < 0, -(u & 0x7FFFFFFF), u)


def _proj_attn_kernel(x_ref, g_ref, w_ref, q_ref, k_ref, v_ref, kt_ref, vt_ref, qi_ref, tail_ref, kit_ref,
                      hl_ref, *, tm, hl_rows):
    h = _rms(x_ref[...], g_ref[...])
    hl_ref[...] = h[tm - hl_rows:, :]
    hb = h.astype(BF16)
    q = _dot(hb, w_ref[:, 0:512]) * ATT_SCALE
    for hd in range(N_HEADS_A):
        q_ref[hd] = q[:, hd * 64:(hd + 1) * 64].astype(BF16)
    k = _dot(hb, w_ref[:, 512:1024])
    k_ref[...] = k
    kt_ref[0] = k.T.astype(BF16)
    v = _dot(hb, w_ref[:, 1024:1536])
    v_ref[...] = v
    vt_ref[0] = v.T.astype(BF16)
    qi = _dot(hb, w_ref[:, 1536:1792])
    for hd in range(IDX_HEADS):
        qi_ref[hd] = qi[:, hd * 64:(hd + 1) * 64].astype(BF16)
    tail = _dot(hb, w_ref[:, 1792:1920])
    tail_ref[...] = tail
    kit_ref[0] = tail.T[0:64, :].astype(BF16)


def _proj_attn(x2d, g, w_attn, tm, hl_rows):
    n = x2d.shape[0]
    nt = n // tm
    row = lambda i: (i, 0)
    return pl.pallas_call(
        functools.partial(_proj_attn_kernel, tm=tm, hl_rows=hl_rows),
        grid=(nt,),
        in_specs=[pl.BlockSpec((tm, D_MODEL), row),
                  pl.BlockSpec((1, D_MODEL), lambda i: (0, 0)),
                  _resident((D_MODEL, 1920), lambda i: (0, 0))],
        out_specs=[pl.BlockSpec((N_HEADS_A, tm, 64), lambda i: (0, i, 0)),
                   pl.BlockSpec((tm, 512), row),
                   pl.BlockSpec((tm, 512), row),
                   pl.BlockSpec((1, 512, tm), lambda i: (i, 0, 0)),
                   pl.BlockSpec((1, 512, tm), lambda i: (i, 0, 0)),
                   pl.BlockSpec((IDX_HEADS, tm, 64), lambda i: (0, i, 0)),
                   pl.BlockSpec((tm, 128), row),
                   pl.BlockSpec((1, 64, tm), lambda i: (i, 0, 0)),
                   pl.BlockSpec((hl_rows, D_MODEL), row)],
        out_shape=[jax.ShapeDtypeStruct((N_HEADS_A, n, 64), BF16),
                   jax.ShapeDtypeStruct((n, 512), F32),
                   jax.ShapeDtypeStruct((n, 512), F32),
                   jax.ShapeDtypeStruct((nt, 512, tm), BF16),
                   jax.ShapeDtypeStruct((nt, 512, tm), BF16),
                   jax.ShapeDtypeStruct((IDX_HEADS, n, 64), BF16),
                   jax.ShapeDtypeStruct((n, 128), F32),
                   jax.ShapeDtypeStruct((nt, 64, tm), BF16),
                   jax.ShapeDtypeStruct((nt * hl_rows, D_MODEL), F32)],
        compiler_params=_cparams(1),
        name="proj_attn",
    )(x2d, g, w_attn)


def _proj_rest_kernel(x_ref, g_ref, w_ref, rw_ref, gate_ref):
    hb = _rms(x_ref[...], g_ref[...]).astype(BF16)
    rw_ref[...] = _dot(hb, w_ref[:, 0:RWKV_COLS])
    gate_ref[...] = _dot(hb, w_ref[:, RWKV_COLS:RWKV_COLS + 2 * D_MODEL])


def _proj_rest(x2d, g, w_rest, tm):
    n = x2d.shape[0]
    row = lambda i: (i, 0)
    return pl.pallas_call(
        _proj_rest_kernel,
        grid=(n // tm,),
        in_specs=[pl.BlockSpec((tm, D_MODEL), row),
                  pl.BlockSpec((1, D_MODEL), lambda i: (0, 0)),
                  _resident((D_MODEL, RWKV_COLS + 2 * D_MODEL), lambda i: (0, 0))],
        out_specs=[pl.BlockSpec((tm, RWKV_COLS), row),
                   pl.BlockSpec((tm, 2 * D_MODEL), row)],
        out_shape=[jax.ShapeDtypeStruct((n, RWKV_COLS), F32),
                   jax.ShapeDtypeStruct((n, 2 * D_MODEL), F32)],
        compiler_params=_cparams(1),
        name="proj_rest",
    )(x2d, g, w_rest)


def _shift_proj_kernel(s_ref, w_ref, o_ref):
    o_ref[...] = _dot(s_ref[...].astype(BF16), w_ref[...])


def _shift_proj(shift_rows, w_rw):
    b = shift_rows.shape[0]
    return pl.pallas_call(
        _shift_proj_kernel,
        out_shape=jax.ShapeDtypeStruct((b, RWKV_COLS), F32),
        compiler_params=pltpu.CompilerParams(vmem_limit_bytes=VMEM_LIMIT),
        name="shift_proj",
    )(shift_rows, w_rw)


def _bias_delta(btab, rb_ref, hd):
    out = jnp.zeros(btab.shape, F32)
    far = rb_ref[REL_BUCKETS - 1, hd]
    for j in range(REL_BUCKETS - 1):
        out = jnp.where(btab == j, rb_ref[j, hd] - far, out)
    return out


def _tri_ones():
    r = lax.broadcasted_iota(I32, (LANE, LANE), 0)
    c = lax.broadcasted_iota(I32, (LANE, LANE), 1)
    return jnp.where(r <= c, 1.0, 0.0).astype(BF16), jnp.ones((LANE, LANE), BF16)


def _prompt_attn_kernel(btab_ref, rb_ref, q_ref, qi_ref, tail_ref, kit_ref, kt_ref, vt_ref, o_ref,
                        keyv, msk, lg, dt, wbs, *, topk):
    b = pl.program_id(0)
    i = pl.program_id(1)
    jpc = KC // LANE
    nc = i // jpc + 1

    @pl.when((b == 0) & (i == 0))
    def _():
        bt = btab_ref[...]
        for hd in range(N_HEADS_A):
            dt[hd] = _bias_delta(bt, rb_ref, hd)

    row = lax.broadcasted_iota(I32, (QB, LANE), 0) + i * QB
    lane = lax.broadcasted_iota(I32, (QB, LANE), 1)

    w4 = tail_ref[:, 64:68] * IDX_SCALE
    for hd in range(IDX_HEADS):
        wbs[hd] = jnp.broadcast_to(w4[:, hd:hd + 1], (QB, LANE))
    q4 = qi_ref[...].reshape(IDX_HEADS * QB, IDX_DIM)

    def score_chunk(c, carry):
        d = _dot(q4, kit_ref[c])
        for j in range(jpc):
            s = jnp.zeros((QB, LANE), F32)
            for hd in range(IDX_HEADS):
                s = s + jnp.maximum(d[hd * QB:(hd + 1) * QB, j * LANE:(j + 1) * LANE], 0.0) * wbs[hd]
            kp = c * KC + j * LANE + lane
            keyv[c * jpc + j] = _sort_key(jnp.where(kp <= row, s, NEG_INF))
        return carry

    lax.fori_loop(0, nc, score_chunk, 0)

    def count_ge(cand):
        def body(c, cnt):
            for j in range(jpc):
                cnt = cnt + jnp.where(keyv[c * jpc + j] >= cand, 1.0, 0.0)
            return cnt
        cnt = lax.fori_loop(0, nc, body, jnp.zeros((QB, LANE), F32))
        return jnp.sum(cnt, axis=-1, keepdims=True)

    def bit_step(it, t_u):
        cand_u = t_u | jnp.left_shift(jnp.int32(1), 31 - it)
        n_ge = count_ge(cand_u ^ INT_MIN)
        return jnp.where(n_ge >= float(topk), cand_u, t_u)

    t_u = lax.fori_loop(0, 32, bit_step, jnp.zeros((QB, LANE), I32))
    thr = t_u ^ INT_MIN

    def count_gt(c, cnt):
        for j in range(jpc):
            cnt = cnt + jnp.where(keyv[c * jpc + j] > thr, 1.0, 0.0)
        return cnt
    n_gt = jnp.sum(lax.fori_loop(0, nc, count_gt, jnp.zeros((QB, LANE), F32)), axis=-1, keepdims=True)
    need = float(topk) - n_gt
    tri, ones = _tri_ones()

    def mask_chunk(c, off):
        for j in range(jpc):
            kv = keyv[c * jpc + j]
            eq = kv == thr
            eqb = jnp.where(eq, 1.0, 0.0).astype(BF16)
            pre = off + _dot(eqb, tri)
            kp = c * KC + j * LANE + lane
            sel = ((kv > thr) | (eq & (pre <= need))) & (kp <= row)
            msk[c * jpc + j] = jnp.where(sel, 0.0, NEG_INF)
            off = off + _dot(eqb, ones)
        return off

    lax.fori_loop(0, nc, mask_chunk, jnp.zeros((QB, LANE), F32))

    for hd in range(N_HEADS_A):
        qh = q_ref[hd]
        hs = slice(hd * HEAD_DIM, (hd + 1) * HEAD_DIM)

        def logit_chunk(c, m, near):
            lgt = _dot(qh, kt_ref[c, hs, :])
            for j in range(jpc):
                blk = c * jpc + j
                x = lgt[:, j * LANE:(j + 1) * LANE] + msk[blk]
                if near:
                    x = x + jnp.where(blk == i, dt[hd, :, LANE:], jnp.where(blk == i - 1, dt[hd, :, :LANE], 0.0))
                lg[blk] = x
                m = jnp.maximum(m, x)
            return m

        n_far = jnp.maximum(nc - 2, 0)
        m = lax.fori_loop(0, n_far, functools.partial(logit_chunk, near=False),
                          jnp.full((QB, LANE), NEG_INF, F32))
        m = lax.fori_loop(n_far, nc, functools.partial(logit_chunk, near=True), m)
        mrow = jnp.max(m, axis=-1, keepdims=True)

        def pv_chunk(c, carry):
            l, acc = carry
            ps = []
            for j in range(jpc):
                p = jnp.exp(lg[c * jpc + j] - mrow)
                l = l + p
                ps.append(p.astype(BF16))
            acc = acc + _dot_t(jnp.concatenate(ps, axis=1), vt_ref[c, hs, :])
            return l, acc

        l, acc = lax.fori_loop(0, nc, pv_chunk, (jnp.zeros((QB, LANE), F32), jnp.zeros((QB, HEAD_DIM), F32)))
        o_ref[:, hs] = acc / jnp.sum(l, axis=-1, keepdims=True)


def _prompt_attn(btab, rel_bias, q_hm, qi_hm, tail, kit, kt, vt, batch, seq):
    nq = seq // QB
    ncb = seq // KC
    nkb = seq // LANE
    topk = min(TOPK_MAX, seq // 4)
    return pl.pallas_call(
        functools.partial(_prompt_attn_kernel, topk=topk),
        grid=(batch, nq),
        in_specs=[pl.BlockSpec((QB, 2 * QB), lambda b, i: (0, 0)),
                  pl.BlockSpec(memory_space=pltpu.SMEM),
                  pl.BlockSpec((N_HEADS_A, QB, 64), lambda b, i: (0, b * nq + i, 0)),
                  pl.BlockSpec((IDX_HEADS, QB, 64), lambda b, i: (0, b * nq + i, 0)),
                  pl.BlockSpec((QB, 128), lambda b, i: (b * nq + i, 0)),
                  _resident((ncb, 64, KC), lambda b, i: (b, 0, 0)),
                  _resident((ncb, 512, KC), lambda b, i: (b, 0, 0)),
                  _resident((ncb, 512, KC), lambda b, i: (b, 0, 0))],
        out_specs=pl.BlockSpec((QB, A_WIDTH), lambda b, i: (b * nq + i, 0)),
        out_shape=jax.ShapeDtypeStruct((batch * seq, A_WIDTH), F32),
        scratch_shapes=[pltpu.VMEM((nkb, QB, LANE), I32),
                        pltpu.VMEM((nkb, QB, LANE), F32),
                        pltpu.VMEM((nkb, QB, LANE), F32),
                        pltpu.VMEM((N_HEADS_A, QB, 2 * QB), F32),
                        pltpu.VMEM((IDX_HEADS, QB, LANE), F32)],
        compiler_params=_cparams(2),
        name="prompt_attn",
    )(btab, rel_bias, q_hm, qi_hm, tail, kit, kt, vt)


SEL_PAGES = 16
ATT_PAGES = 8


def _sample_select_kernel(pt_ref, qi_ref, tail_ref, *rest, n_groups, t_new, topk):
    pages = rest[:SEL_PAGES]
    mpast_ref, mnew_ref, keyv, keyn = rest[SEL_PAGES:]
    g = pl.program_id(1)
    gw = SEL_PAGES * PAGE
    w4 = tail_ref[:, 64:68] * IDX_SCALE
    q4 = qi_ref[...].reshape(IDX_HEADS * t_new, IDX_DIM)

    def scores(d):
        s = jnp.zeros((t_new, d.shape[1]), F32)
        for hd in range(IDX_HEADS):
            s = s + jnp.maximum(d[hd * t_new:(hd + 1) * t_new, :], 0.0) * w4[:, hd:hd + 1]
        return s

    kpast_t = jnp.concatenate([p[0] for p in pages], axis=1).astype(BF16)
    keyv[g] = _sort_key(scores(_dot(q4, kpast_t)))

    @pl.when(g == n_groups - 1)
    def _():
        rown = lax.broadcasted_iota(I32, (t_new, LANE), 0)
        lanen = lax.broadcasted_iota(I32, (t_new, LANE), 1)
        knew = jnp.concatenate([tail_ref[:, 0:64], jnp.zeros((LANE - t_new, 64), F32)], axis=0).astype(BF16)
        sn = scores(_dot_t(q4, knew))
        causal_n = lanen <= rown
        keyn[...] = _sort_key(jnp.where(causal_n, sn, NEG_INF))

        def count(cmp):
            cnt = jnp.zeros((t_new, LANE), F32)
            for gg in range(n_groups):
                hit = jnp.where(cmp(keyv[gg]), 1.0, 0.0)
                for j in range(gw // LANE):
                    cnt = cnt + hit[:, j * LANE:(j + 1) * LANE]
            cnt = cnt + jnp.where(cmp(keyn[...]), 1.0, 0.0)
            return jnp.sum(cnt, axis=-1, keepdims=True)

        def bit_step(it, t_u):
            cand_u = t_u | jnp.left_shift(jnp.int32(1), 31 - it)
            cand = cand_u ^ INT_MIN
            n_ge = count(lambda kv: kv >= cand)
            return jnp.where(n_ge >= float(topk), cand_u, t_u)

        t_u = lax.fori_loop(0, 32, bit_step, jnp.zeros((t_new, 1), I32))
        thr = t_u ^ INT_MIN
        need = float(topk) - count(lambda kv: kv > thr)
        tri, ones = _tri_ones()
        off = jnp.zeros((t_new, LANE), F32)
        for gg in range(n_groups):
            kvg = keyv[gg]
            for j in range(gw // LANE):
                kv = kvg[:, j * LANE:(j + 1) * LANE]
                eq = kv == thr
                eqb = jnp.where(eq, 1.0, 0.0).astype(BF16)
                sel = (kv > thr) | (eq & (off + _dot(eqb, tri) <= need))
                mpast_ref[0, :, gg * gw + j * LANE:gg * gw + (j + 1) * LANE] = jnp.where(sel, 0.0, NEG_INF)
                off = off + _dot(eqb, ones)
        kv = keyn[...]
        eq = kv == thr
        eqb = jnp.where(eq, 1.0, 0.0).astype(BF16)
        sel = ((kv > thr) | (eq & (off + _dot(eqb, tri) <= need))) & causal_n
        mnew_ref[0] = jnp.where(sel, 0.0, NEG_INF)


def _sample_select(page_table, qi_hm, tail, cache_idx_t, t_new):
    bsz, n_pages = page_table.shape
    n_groups = n_pages // SEL_PAGES
    past = n_pages * PAGE
    topk = min(TOPK_MAX, (past + t_new) // 4)

    def page_spec(j):
        return pl.BlockSpec((1, IDX_DIM, PAGE), lambda b, g, pt: (pt[b, g * SEL_PAGES + j], 0, 0))

    grid_spec = pltpu.PrefetchScalarGridSpec(
        num_scalar_prefetch=1,
        grid=(bsz, n_groups),
        in_specs=[pl.BlockSpec((IDX_HEADS, t_new, 64), lambda b, g, pt: (0, b, 0)),
                  pl.BlockSpec((t_new, 128), lambda b, g, pt: (b, 0))]
                 + [page_spec(j) for j in range(SEL_PAGES)],
        out_specs=[pl.BlockSpec((1, t_new, past), lambda b, g, pt: (b, 0, 0)),
                   pl.BlockSpec((1, t_new, LANE), lambda b, g, pt: (b, 0, 0))],
        scratch_shapes=[pltpu.VMEM((n_groups, t_new, SEL_PAGES * PAGE), I32),
                        pltpu.VMEM((t_new, LANE), I32)])
    return pl.pallas_call(
        functools.partial(_sample_select_kernel, n_groups=n_groups, t_new=t_new, topk=topk),
        grid_spec=grid_spec,
        out_shape=[jax.ShapeDtypeStruct((bsz, t_new, past), F32),
                   jax.ShapeDtypeStruct((bsz, t_new, LANE), F32)],
        compiler_params=_cparams(2),
        name="sample_select",
    )(page_table, qi_hm, tail, *([cache_idx_t] * SEL_PAGES))


def _sample_attn_kernel(pt_ref, btab_ref, rb_ref, q_ref, knew_ref, vnew_ref, mpast_ref, mnew_ref, *rest,
                        n_steps, t_new):
    kpages = rest[:ATT_PAGES]
    vpages = rest[ATT_PAGES:2 * ATT_PAGES]
    o_ref, qbd, dq, m_s, l_s, acc_s = rest[2 * ATT_PAGES:]
    g = pl.program_id(1)
    hq = N_HEADS_A * t_new
    gw = ATT_PAGES * PAGE
    head_of_row = lax.broadcasted_iota(I32, (hq, A_WIDTH), 0) // t_new
    head_of_col = lax.broadcasted_iota(I32, (hq, A_WIDTH), 1) // HEAD_DIM
    own = head_of_row == head_of_col

    @pl.when((pl.program_id(0) == 0) & (g == 0))
    def _():
        bt = btab_ref[0:t_new, :]
        for hd in range(N_HEADS_A):
            dq[hd * t_new:(hd + 1) * t_new, :] = _bias_delta(bt, rb_ref, hd)

    @pl.when(g == 0)
    def _():
        qt = jnp.concatenate([q_ref[...].astype(F32)] * N_HEADS_A, axis=0)
        qbd[...] = jnp.where(own, qt, 0.0).astype(BF16)
        m_s[...] = jnp.full((hq, 1), NEG_INF, F32)
        l_s[...] = jnp.zeros((hq, 1), F32)
        acc_s[...] = jnp.zeros((hq, A_WIDTH), F32)

    def online(x, vb_t):
        m_old = m_s[...]
        m_new = jnp.maximum(m_old, jnp.max(x, axis=-1, keepdims=True))
        m_fin = jnp.where(m_new == NEG_INF, 0.0, m_new)
        alpha = jnp.exp(m_old - m_fin)
        p = jnp.exp(x - m_fin)
        l_s[...] = alpha * l_s[...] + jnp.sum(p, axis=-1, keepdims=True)
        acc_s[...] = alpha * acc_s[...] + _dot_t(p.astype(BF16), vb_t)
        m_s[...] = m_new

    kb_t = jnp.concatenate([p[0] for p in kpages], axis=1).astype(BF16)
    vb_t = jnp.concatenate([p[0] for p in vpages], axis=1).astype(BF16)
    x = _dot(qbd[...], kb_t) + jnp.concatenate([mpast_ref[0]] * N_HEADS_A, axis=0)
    lane = lax.broadcasted_iota(I32, (hq, gw), 1)
    near = jnp.concatenate([jnp.zeros((hq, gw - LANE), F32), dq[:, :LANE]], axis=1)
    x = x + jnp.where(g == n_steps - 1, near, 0.0)

    online(x, vb_t)

    @pl.when(g == n_steps - 1)
    def _():
        pad = jnp.zeros((LANE - t_new, A_WIDTH), F32)
        kn = jnp.concatenate([knew_ref[...], pad], axis=0).astype(BF16)
        vn = jnp.concatenate([vnew_ref[...], pad], axis=0)
        xn = _dot_t(qbd[...], kn) + jnp.concatenate([mnew_ref[0]] * N_HEADS_A, axis=0) + dq[:, LANE:]
        online(xn, vn.T.astype(BF16))
        res = jnp.where(own, acc_s[...] / l_s[...], 0.0)
        out = res[0:t_new, :]
        for hd in range(1, N_HEADS_A):
            out = out + res[hd * t_new:(hd + 1) * t_new, :]
        o_ref[...] = out


def _sample_attn(page_table, btab, rel_bias, q_rows, k_new, v_new, mpast, mnew, cache_kt, cache_vt, t_new):
    bsz, n_pages = page_table.shape
    n_steps = n_pages // ATT_PAGES
    hq = N_HEADS_A * t_new

    def page_spec(j):
        return pl.BlockSpec((1, A_WIDTH, PAGE), lambda b, g, pt: (pt[b, g * ATT_PAGES + j], 0, 0))

    grid_spec = pltpu.PrefetchScalarGridSpec(
        num_scalar_prefetch=1,
        grid=(bsz, n_steps),
        in_specs=[pl.BlockSpec((QB, 2 * QB), lambda b, g, pt: (0, 0)),
                  pl.BlockSpec(memory_space=pltpu.SMEM),
                  pl.BlockSpec((t_new, A_WIDTH), lambda b, g, pt: (b, 0)),
                  pl.BlockSpec((t_new, A_WIDTH), lambda b, g, pt: (b, 0)),
                  pl.BlockSpec((t_new, A_WIDTH), lambda b, g, pt: (b, 0)),
                  pl.BlockSpec((1, t_new, ATT_PAGES * PAGE), lambda b, g, pt: (b, 0, g)),
                  pl.BlockSpec((1, t_new, LANE), lambda b, g, pt: (b, 0, 0))]
                 + [page_spec(j) for j in range(ATT_PAGES)] * 2,
        out_specs=pl.BlockSpec((t_new, A_WIDTH), lambda b, g, pt: (b, 0)),
        scratch_shapes=[pltpu.VMEM((hq, A_WIDTH), BF16),
                        pltpu.VMEM((hq, 2 * QB), F32),
                        pltpu.VMEM((hq, 1), F32),
                        pltpu.VMEM((hq, 1), F32),
                        pltpu.VMEM((hq, A_WIDTH), F32)])
    return pl.pallas_call(
        functools.partial(_sample_attn_kernel, n_steps=n_steps, t_new=t_new),
        grid_spec=grid_spec,
        out_shape=jax.ShapeDtypeStruct((bsz * t_new, A_WIDTH), F32),
        compiler_params=_cparams(2),
        name="sample_attn",
    )(page_table, btab, rel_bias, q_rows, k_new, v_new, mpast, mnew,
      *([cache_kt] * ATT_PAGES), *([cache_vt] * ATT_PAGES))


def _head_block_ones():
    r = lax.broadcasted_iota(I32, (B_WIDTH, B_WIDTH), 0) // HEAD_B
    c = lax.broadcasted_iota(I32, (B_WIDTH, B_WIDTH), 1) // HEAD_B
    return jnp.where(r == c, 1.0, 0.0).astype(BF16)


def _rwkv_prep_kernel(rw_ref, prev8_ref, prow_ref, mu_ref, w0_ref, a0_ref, kk_ref, ka_ref, rk_ref,
                      w2_ref, a2_ref, g2_ref,
                      r_ref, w_ref, k_ref, v_ref, na_ref, bb_ref, bonus_ref, g_ref, *, tt):
    t = pl.program_id(1)
    cur = rw_ref[0]
    first_prev = jnp.where(t == 0, prow_ref[0], prev8_ref[0, 7:8, :])
    rows = lax.broadcasted_iota(I32, cur.shape, 0)
    prev = jnp.where(rows == 0, first_prev, pltpu.roll(cur, 1, 0))
    mixed = cur + (prev - cur) * mu_ref[...]
    r = mixed[:, 0:512]
    k = mixed[:, 512:1024]
    v = mixed[:, 1024:1536]
    wd = mixed[:, 1536:1600]
    ad = mixed[:, 1600:1664]
    gd = mixed[:, 1664:1792]
    z = w0_ref[...] + _dot(jnp.tanh(wd).astype(BF16), w2_ref[...])
    u = -z
    softplus = jnp.maximum(u, 0.0) + jnp.log1p(jnp.exp(-jnp.abs(u)))
    decay = jnp.exp(-jnp.exp(-softplus - 0.5))
    a = jax.nn.sigmoid(a0_ref[...] + _dot(ad.astype(BF16), a2_ref[...]))
    g = _dot(jax.nn.sigmoid(gd).astype(BF16), g2_ref[...])
    bd = _head_block_ones()
    kk = k * kk_ref[...]
    nrm = jnp.sqrt(_split_dot(kk * kk, bd))
    kk = kk / jnp.maximum(nrm, 1e-12)
    k2 = k * (1.0 + (a - 1.0) * ka_ref[...])
    r_ref[0] = r
    w_ref[0] = decay
    k_ref[0] = k2
    v_ref[0] = v
    na_ref[0] = -kk
    bb_ref[0] = kk * a
    bonus_ref[0] = _split_dot(r * k2 * rk_ref[...], bd) * v
    g_ref[0] = g


def _rwkv_prep(rw3, prow, p, tt):
    bsz, t_len, _ = rw3.shape
    vec = lambda c: pl.BlockSpec((1, c), lambda b, t: (0, 0))
    full = lambda r, c: pl.BlockSpec((r, c), lambda b, t: (0, 0))
    tile = pl.BlockSpec((1, tt, B_WIDTH), lambda b, t: (b, t, 0))
    return pl.pallas_call(
        functools.partial(_rwkv_prep_kernel, tt=tt),
        grid=(bsz, t_len // tt),
        in_specs=[pl.BlockSpec((1, tt, RWKV_COLS), lambda b, t: (b, t, 0)),
                  pl.BlockSpec((1, 8, RWKV_COLS), lambda b, t: (b, jnp.maximum(t * (tt // 8) - 1, 0), 0)),
                  pl.BlockSpec((1, 1, RWKV_COLS), lambda b, t: (b, 0, 0)),
                  vec(RWKV_COLS), vec(B_WIDTH), vec(B_WIDTH), vec(B_WIDTH), vec(B_WIDTH), vec(B_WIDTH),
                  full(D_DECAY, B_WIDTH), full(D_AAA, B_WIDTH), full(D_GATE, B_WIDTH)],
        out_specs=[tile] * 8,
        out_shape=[jax.ShapeDtypeStruct((bsz, t_len, B_WIDTH), F32)] * 8,
        compiler_params=_cparams(2),
        name="rwkv_prep",
    )(rw3, rw3, prow.reshape(bsz, 1, RWKV_COLS), p["mu"], p["w0"], p["a0"], p["k_k"], p["k_a"], p["r_k"],
      p["w2"], p["a2"], p["g2"])


SCAN_GROUP = 8


def _rwkv_scan_kernel(r_ref, w_ref, k_ref, v_ref, na_ref, bb_ref, s0_ref, y_ref, st_ref, *, bb_n, tc):
    @pl.when(pl.program_id(1) == 0)
    def _():
        st_ref[...] = s0_ref[...]

    lane = lax.broadcasted_iota(I32, (HEAD_B, LANE), 1)
    rowi = lax.broadcasted_iota(I32, (HEAD_B, LANE), 0)
    diag = (lane & (HEAD_B - 1)) == rowi
    step_lane = lane & (HEAD_B - 1)
    bj = lax.broadcasted_iota(I32, (2 * LANE, LANE), 0)
    bl = lax.broadcasted_iota(I32, (2 * LANE, LANE), 1)
    bd2 = jnp.where((bj & (LANE - 1)) // HEAD_B == bl // HEAD_B, 1.0, 0.0).astype(BF16)
    bd1 = bd2[0:LANE]

    def hi_lo(x):
        hi = x.astype(BF16)
        return hi, (x - hi.astype(F32)).astype(BF16)

    chains = [(bi, pr) for bi in range(bb_n) for pr in range(N_HEADS_B // 2)]
    groups = [chains[g:g + SCAN_GROUP] for g in range(0, len(chains), SCAN_GROUP)]

    def tile(t8, carry):
        t0 = pl.multiple_of(t8 * 8, 8)
        ins = {c: tuple(ref[c[0], pl.ds(t0, 8), c[1] * LANE:(c[1] + 1) * LANE]
                        for ref in (r_ref, w_ref, k_ref, v_ref, na_ref, bb_ref)) for c in chains}
        state = {c: st_ref[c[0], c[1]] for c in chains}
        ycols = {c: jnp.zeros((HEAD_B, LANE), F32) for c in chains}
        for i in range(8):
            for grp in groups:
                rows = {c: tuple(x[i:i + 1, :] for x in ins[c]) for c in grp}
                p_all, z_all = [], []
                for c in grp:
                    rr, ww, kk, vv, aa, bb = rows[c]
                    p_all.append(jnp.concatenate(hi_lo(state[c] * aa), axis=1))
                    vh, vl = hi_lo(vv)
                    z_all.append(jnp.concatenate([jnp.where(diag, vh.astype(F32), 0.0).astype(BF16),
                                                  jnp.where(diag, vl.astype(F32), 0.0).astype(BF16)], axis=1))
                sab_all = _dot(jnp.concatenate(p_all, axis=0), bd2)
                vc_all = _dot(jnp.concatenate(z_all, axis=0), bd2)
                y_all = []
                for j, c in enumerate(grp):
                    rr, ww, kk, vv, aa, bb = rows[c]
                    js = slice(j * HEAD_B, (j + 1) * HEAD_B)
                    s = state[c] * ww + sab_all[js] * bb + vc_all[js] * kk
                    state[c] = s
                    y_all.append((s * rr).astype(BF16))
                yb_all = _dot(jnp.concatenate(y_all, axis=0), bd1)
                for j, c in enumerate(grp):
                    ycols[c] = jnp.where(step_lane == i, yb_all[j * HEAD_B:(j + 1) * HEAD_B], ycols[c])
        for c in chains:
            st_ref[c[0], c[1]] = state[c]
            yt = ycols[c].T
            y_ref[c[0], pl.ds(t0, 8), c[1] * LANE:c[1] * LANE + HEAD_B] = yt[0:8]
            y_ref[c[0], pl.ds(t0, 8), c[1] * LANE + HEAD_B:(c[1] + 1) * LANE] = yt[HEAD_B:HEAD_B + 8]
        return carry

    lax.fori_loop(0, tc // 8, tile, 0)


def _pair_pack(state):
    b = state.shape[0]
    s = state.reshape(b, N_HEADS_B // 2, 2, HEAD_B, HEAD_B)
    return jnp.transpose(s, (0, 1, 3, 2, 4)).reshape(b, N_HEADS_B // 2, HEAD_B, 2 * HEAD_B)


def _pair_unpack(state):
    b = state.shape[0]
    s = state.reshape(b, N_HEADS_B // 2, HEAD_B, 2, HEAD_B)
    return jnp.transpose(s, (0, 1, 3, 2, 4)).reshape(b, N_HEADS_B, HEAD_B, HEAD_B)


def _rwkv_scan(r, w, k, v, na, bb, state0, bb_n, tc):
    bsz, t_len, _ = r.shape
    tile = pl.BlockSpec((bb_n, tc, B_WIDTH), lambda b, t: (b, t, 0))
    st = pl.BlockSpec((bb_n, N_HEADS_B // 2, HEAD_B, 2 * HEAD_B), lambda b, t: (b, 0, 0, 0))
    y, state = pl.pallas_call(
        functools.partial(_rwkv_scan_kernel, bb_n=bb_n, tc=tc),
        grid=(bsz // bb_n, t_len // tc),
        in_specs=[tile] * 6 + [st],
        out_specs=[tile, st],
        out_shape=[jax.ShapeDtypeStruct((bsz, t_len, B_WIDTH), F32),
                   jax.ShapeDtypeStruct((bsz, N_HEADS_B // 2, HEAD_B, 2 * HEAD_B), F32)],
        compiler_params=_cparams(2),
        name="rwkv_scan",
    )(r, w, k, v, na, bb, _pair_pack(state0))
    return y, _pair_unpack(state)


def _tail_kernel(x_ref, att_ref, y_ref, bonus_ref, g_ref, gate_ref, lnw_ref, lnb_ref, nm_ref, nf_ref,
                 wa_ref, wb_ref, wo_ref, w1_ref, w2_ref, o_ref):
    bd = _head_block_ones()
    y = y_ref[...]
    mean = _split_dot(y, bd) * (1.0 / HEAD_B)
    yc = y - mean
    var = _split_dot(yc * yc, bd) * (1.0 / HEAD_B)
    yn = yc * lax.rsqrt(var + GN_EPS) * lnw_ref[...] + lnb_ref[...]
    rwkv = (yn + bonus_ref[...]) * g_ref[...]
    ga = gate_ref[:, 0:D_MODEL]
    gb = gate_ref[:, D_MODEL:2 * D_MODEL]
    merged = (jax.nn.sigmoid(ga) * _dot(att_ref[...].astype(BF16), wa_ref[...])
              + jax.nn.sigmoid(gb) * _dot(rwkv.astype(BF16), wb_ref[...]))
    x1 = x_ref[...] + _dot(merged.astype(BF16), wo_ref[...])
    hm = _rms(x1, nm_ref[...]).astype(BF16)
    up = jnp.maximum(_dot(hm, w1_ref[...]), 0.0)
    x2 = x1 + _dot((up * up).astype(BF16), w2_ref[...])
    o_ref[...] = _rms(x2, nf_ref[...])


def _tail(x2d, att, y, bonus, g, gates, p, tm):
    n = x2d.shape[0]
    row = lambda i: (i, 0)
    vec = lambda c: pl.BlockSpec((1, c), lambda i: (0, 0))
    res = lambda r, c: _resident((r, c), lambda i: (0, 0))
    return pl.pallas_call(
        _tail_kernel,
        grid=(n // tm,),
        in_specs=[pl.BlockSpec((tm, D_MODEL), row),
                  pl.BlockSpec((tm, A_WIDTH), row), pl.BlockSpec((tm, B_WIDTH), row),
                  pl.BlockSpec((tm, B_WIDTH), row), pl.BlockSpec((tm, B_WIDTH), row),
                  pl.BlockSpec((tm, 2 * D_MODEL), row),
                  vec(B_WIDTH), vec(B_WIDTH), vec(D_MODEL), vec(D_MODEL),
                  res(A_WIDTH, D_MODEL), res(B_WIDTH, D_MODEL), res(D_MODEL, D_MODEL),
                  res(D_MODEL, D_FF), res(D_FF, D_MODEL)],
        out_specs=pl.BlockSpec((tm, D_MODEL), row),
        out_shape=jax.ShapeDtypeStruct((n, D_MODEL), F32),
        compiler_params=_cparams(1),
        name="tail",
    )(x2d, att, y, bonus, g, gates, p["ln_w"], p["ln_b"], p["norm_mlp"], p["norm_final"],
      p["wa"], p["wb"], p["wo"], p["w1"], p["w2m"])


def _prepare_params(norm_mix, w_in, rwkv_mu, rwkv_w0, rwkv_w2, rwkv_a0, rwkv_a2, rwkv_g2, rwkv_k_k, rwkv_k_a,
                    rwkv_r_k, rwkv_ln_w, rwkv_ln_b, w_branch_a, w_branch_b, w_out, norm_mlp, w_mlp_in,
                    w_mlp_out, norm_final):
    wb16 = w_in.astype(BF16)
    o_idx = 3 * A_WIDTH
    o_kidx = o_idx + IDX_HEADS * IDX_DIM
    o_widx = o_kidx + IDX_DIM
    pad = jnp.zeros((D_MODEL, 128 - IDX_DIM - IDX_HEADS), BF16)
    w_attn = jnp.concatenate([wb16[:, :o_kidx], wb16[:, o_kidx:o_widx + IDX_HEADS], pad], axis=1)
    o = RWKV_OFF
    def regroup(a):
        return jnp.concatenate([a[..., 0:512], a[..., 576:1600], a[..., 512:576], a[..., 1600:1792]], axis=-1)
    w_rw = regroup(wb16[:, o:o + RWKV_COLS])
    w_rest = jnp.concatenate([w_rw, wb16[:, o + RWKV_COLS:]], axis=1)
    r1 = lambda a: a.reshape(1, -1).astype(F32)
    return dict(
        norm_mix=r1(norm_mix), w_attn=w_attn, w_rest=w_rest, w_rw=w_rw,
        mu=r1(regroup(rwkv_mu)), w0=r1(rwkv_w0), a0=r1(rwkv_a0), k_k=r1(rwkv_k_k), k_a=r1(rwkv_k_a),
        r_k=r1(rwkv_r_k), w2=rwkv_w2.astype(BF16), a2=rwkv_a2.astype(BF16), g2=rwkv_g2.astype(BF16),
        ln_w=r1(rwkv_ln_w), ln_b=r1(rwkv_ln_b), norm_mlp=r1(norm_mlp), norm_final=r1(norm_final),
        wa=w_branch_a.astype(BF16), wb=w_branch_b.astype(BF16), wo=w_out.astype(BF16),
        w1=w_mlp_in.astype(BF16), w2m=w_mlp_out.astype(BF16))


def _layer(x, shift_rows, wkv0, attend, p, tm, tm_tail, scan_tc):
    bsz, t_len, _ = x.shape
    n = bsz * t_len
    x2d = x.reshape(n, D_MODEL)
    hl_rows = 8 if t_len >= tm else tm
    q_hm, k, v, kt, vt, qi_hm, tail, kit, hlast = _proj_attn(x2d, p["norm_mix"], p["w_attn"], tm, hl_rows)
    rw, gates = _proj_rest(x2d, p["norm_mix"], p["w_rest"], tm)
    att = attend(q_hm, k, v, kt, vt, qi_hm, tail, kit)
    prow = _shift_proj(shift_rows, p["w_rw"])
    r, w, k2, v2, na, bb, bonus, g = _rwkv_prep(rw.reshape(bsz, t_len, RWKV_COLS), prow, p, min(256, t_len))
    y, wkv = _rwkv_scan(r, w, k2, v2, na, bb, wkv0, 2, scan_tc)
    flat = lambda a: a.reshape(n, B_WIDTH)
    out = _tail(x2d, att, flat(y), flat(bonus), flat(g), gates, p, tm_tail)
    if t_len >= tm:
        shift = hlast.reshape(bsz, t_len // tm, 8, D_MODEL)[:, -1, 7]
    else:
        shift = hlast.reshape(bsz, t_len, D_MODEL)[:, -1]
    return (out.reshape(bsz, t_len, D_MODEL), k.reshape(bsz, t_len, N_HEADS_A, HEAD_DIM),
            v.reshape(bsz, t_len, N_HEADS_A, HEAD_DIM), tail[:, :IDX_DIM].reshape(bsz, t_len, IDX_DIM), wkv, shift)


def kernel(x_prompt, x_sample, cache_k, cache_v, cache_idx_k, state_wkv, state_shift, page_table, rel_bias, norm_mix, w_in, rwkv_mu, rwkv_w0, rwkv_w2, rwkv_a0, rwkv_a2, rwkv_g2, rwkv_k_k, rwkv_k_a, rwkv_r_k, rwkv_ln_w, rwkv_ln_b, w_branch_a, w_branch_b, w_out, norm_mlp, w_mlp_in, w_mlp_out, norm_final):
    p = _prepare_params(norm_mix, w_in, rwkv_mu, rwkv_w0, rwkv_w2, rwkv_a0, rwkv_a2, rwkv_g2, rwkv_k_k,
                        rwkv_k_a, rwkv_r_k, rwkv_ln_w, rwkv_ln_b, w_branch_a, w_branch_b, w_out, norm_mlp,
                        w_mlp_in, w_mlp_out, norm_final)
    btab = jnp.asarray(_window_buckets())
    b_p, s_p, _ = x_prompt.shape
    b_s, t_s, _ = x_sample.shape
    n_pool = cache_k.shape[0]

    def prompt_attend(q_hm, k, v, kt, vt, qi_hm, tail, kit):
        return _prompt_attn(btab, rel_bias, q_hm, qi_hm, tail, kit, kt, vt, b_p, s_p)

    def sample_attend(q_hm, k, v, kt, vt, qi_hm, tail, kit):
        key_minor = lambda c: jnp.transpose(c.reshape(n_pool, PAGE, -1), (0, 2, 1))
        mpast, mnew = _sample_select(page_table, qi_hm, tail, key_minor(cache_idx_k), t_s)
        q_rows = jnp.transpose(q_hm, (1, 0, 2)).reshape(b_s * t_s, A_WIDTH)
        return _sample_attn(page_table, btab, rel_bias, q_rows, k, v, mpast, mnew,
                            key_minor(cache_k), key_minor(cache_v), t_s)

    zero_shift = jnp.zeros((b_p, D_MODEL), F32)
    zero_wkv = jnp.zeros((b_p, N_HEADS_B, HEAD_B, HEAD_B), F32)
    y_p, k_p, v_p, ik_p, wkv_p, sh_p = _layer(x_prompt, zero_shift, zero_wkv, prompt_attend, p, KC, 256, 256)
    y_s, k_s, v_s, ik_s, wkv_s, sh_s = _layer(x_sample, state_shift, state_wkv, sample_attend, p,
                                              b_s * t_s, b_s * t_s, t_s)
    return (y_p, y_s, k_p, v_p, ik_p, wkv_p, sh_p, k_s, v_s, ik_s, wkv_s, sh_s)
```

```python
import functools
import math

import numpy as np
import jax
import jax.numpy as jnp
from jax import lax
from jax.experimental import pallas as pl
from jax.experimental.pallas import tpu as pltpu

F32 = jnp.float32
BF16 = jnp.bfloat16
I32 = jnp.int32

D_MODEL = 1024
PAGE = 128
HEAD_DIM = 64
A_WIDTH = 512
N_HEADS_A = 8
IDX_HEADS = 4
IDX_DIM = 64
IDX_SCALE = (IDX_HEADS * IDX_DIM) ** -0.5
ATT_SCALE = HEAD_DIM ** -0.5
TOPK_MAX = 256
REL_BUCKETS = 32
REL_MAX_DIST = 128
HEAD_B = 64
B_WIDTH = 512
N_HEADS_B = 8
D_DECAY = 64
D_AAA = 64
D_GATE = 128
GN_EPS = 64e-5
D_FF = 4096
RMS_EPS = 1e-6
RWKV_COLS = 3 * B_WIDTH + D_DECAY + D_AAA + D_GATE
RWKV_OFF = 3 * A_WIDTH + IDX_HEADS * IDX_DIM + IDX_DIM + IDX_HEADS

LANE = 128
QB = 128
PQB = 256
KC = 512
ATT_SUB = KC
NEG_INF = float("-inf")
INT_MIN = -(2 ** 31)
VMEM_LIMIT = 56 * 1024 * 1024


def _cparams(n_axes):
    return pltpu.CompilerParams(dimension_semantics=("arbitrary",) * n_axes,
                                vmem_limit_bytes=VMEM_LIMIT)


def _resident(shape, index_map):
    return pl.BlockSpec(shape, index_map, pipeline_mode=pl.Buffered(1))


def _bucket_table(dist):
    dist = np.asarray(dist, np.int64)
    max_exact = REL_BUCKETS // 2
    d_f = np.maximum(dist, max_exact).astype(np.float32)
    large = max_exact + (np.log(d_f / np.float32(max_exact)) / np.float32(math.log(REL_MAX_DIST / max_exact))
                         * np.float32(REL_BUCKETS - max_exact)).astype(np.int32)
    large = np.minimum(large, REL_BUCKETS - 1)
    return np.where(dist < max_exact, dist, large).astype(np.int32)


def _window_buckets():
    t = np.arange(QB)[:, None]
    c = np.arange(2 * QB)[None, :]
    return _bucket_table(np.maximum(QB + t - c, 0))


def _window_buckets_t():
    k = np.arange(2 * PQB)[:, None]
    q = np.arange(PQB)[None, :]
    return _bucket_table(np.maximum(PQB + q - k, 0))


def _rms(x, g):
    return x * lax.rsqrt(jnp.mean(x * x, axis=-1, keepdims=True) + RMS_EPS) * g


def _dot(a, b):
    return jnp.dot(a, b, preferred_element_type=F32)


def _dot_t(a, b):
    return lax.dot_general(a, b, (((1,), (1,)), ((), ())), preferred_element_type=F32)


def _split_dot(x, m_bf16):
    hi = x.astype(BF16)
    lo = (x - hi.astype(F32)).astype(BF16)
    return _dot(hi, m_bf16) + _dot(lo, m_bf16)


def _kth_largest(count_ge, k, shape):
    def bit_step(it, t_u):
        cand_u = t_u | jnp.left_shift(jnp.int32(1), 31 - it)
        return jnp.where(count_ge(_ordered_f32(cand_u)) >= float(k), cand_u, t_u)
    return _ordered_f32(lax.fori_loop(0, 32, bit_step, jnp.zeros(shape, I32)))


def _ordered_f32(u):
    s = u ^ INT_MIN
    bits = jnp.where(s < 0, INT_MIN | (-s), s)
    return pltpu.bitcast(bits, F32)


def _proj_attn_kernel(x_ref, g_ref, w_ref, q_ref, k_ref, v_ref, qt_ref, kb_ref, vt_ref, qi_ref, qit_ref,
                      tail_ref, tailb_ref, wt_ref, hl_ref, *, tm, hl_rows):
    h = _rms(x_ref[...], g_ref[...])
    hl_ref[...] = h[tm - hl_rows:, :]
    hb = h.astype(BF16)
    q = _dot(hb, w_ref[:, 0:512]) * ATT_SCALE
    for hd in range(N_HEADS_A):
        q_ref[hd] = q[:, hd * 64:(hd + 1) * 64].astype(BF16)
    qt_ref[0] = q.T.astype(BF16)
    k = _dot(hb, w_ref[:, 512:1024])
    k_ref[...] = k
    kb_ref[...] = k.astype(BF16)
    v = _dot(hb, w_ref[:, 1024:1536])
    v_ref[...] = v
    vt_ref[0] = v.T.astype(BF16)
    qi = _dot(hb, w_ref[:, 1536:1792])
    for hd in range(IDX_HEADS):
        qi_ref[hd] = qi[:, hd * 64:(hd + 1) * 64].astype(BF16)
    qit_ref[0] = qi.T.astype(BF16)
    tail = _dot(hb, w_ref[:, 1792:1920])
    tail_ref[...] = tail
    tailb_ref[...] = tail.astype(BF16)
    wt_ref[0] = tail.T[IDX_DIM:IDX_DIM + 8, :]


def _proj_attn(x2d, g, w_attn, tm, hl_rows):
    n = x2d.shape[0]
    nt = n // tm
    row = lambda i: (i, 0)
    return pl.pallas_call(
        functools.partial(_proj_attn_kernel, tm=tm, hl_rows=hl_rows),
        grid=(nt,),
        in_specs=[pl.BlockSpec((tm, D_MODEL), row),
                  pl.BlockSpec((1, D_MODEL), lambda i: (0, 0)),
                  _resident((D_MODEL, 1920), lambda i: (0, 0))],
        out_specs=[pl.BlockSpec((N_HEADS_A, tm, 64), lambda i: (0, i, 0)),
                   pl.BlockSpec((tm, 512), row),
                   pl.BlockSpec((tm, 512), row),
                   pl.BlockSpec((1, 512, tm), lambda i: (i, 0, 0)),
                   pl.BlockSpec((tm, 512), row),
                   pl.BlockSpec((1, 512, tm), lambda i: (i, 0, 0)),
                   pl.BlockSpec((IDX_HEADS, tm, 64), lambda i: (0, i, 0)),
                   pl.BlockSpec((1, 256, tm), lambda i: (i, 0, 0)),
                   pl.BlockSpec((tm, 128), row),
                   pl.BlockSpec((tm, 128), row),
                   pl.BlockSpec((1, 8, tm), lambda i: (i, 0, 0)),
                   pl.BlockSpec((hl_rows, D_MODEL), row)],
        out_shape=[jax.ShapeDtypeStruct((N_HEADS_A, n, 64), BF16),
                   jax.ShapeDtypeStruct((n, 512), F32),
                   jax.ShapeDtypeStruct((n, 512), F32),
                   jax.ShapeDtypeStruct((nt, 512, tm), BF16),
                   jax.ShapeDtypeStruct((n, 512), BF16),
                   jax.ShapeDtypeStruct((nt, 512, tm), BF16),
                   jax.ShapeDtypeStruct((IDX_HEADS, n, 64), BF16),
                   jax.ShapeDtypeStruct((nt, 256, tm), BF16),
                   jax.ShapeDtypeStruct((n, 128), F32),
                   jax.ShapeDtypeStruct((n, 128), BF16),
                   jax.ShapeDtypeStruct((nt, 8, tm), F32),
                   jax.ShapeDtypeStruct((nt * hl_rows, D_MODEL), F32)],
        compiler_params=_cparams(1),
        name="proj_attn",
    )(x2d, g, w_attn)


def _proj_rest_kernel(x_ref, g_ref, w_ref, rw_ref, gate_ref):
    hb = _rms(x_ref[...], g_ref[...]).astype(BF16)
    rw_ref[...] = _dot(hb, w_ref[:, 0:RWKV_COLS])
    gate_ref[...] = _dot(hb, w_ref[:, RWKV_COLS:RWKV_COLS + 2 * D_MODEL])


def _proj_rest(x2d, g, w_rest, tm):
    n = x2d.shape[0]
    row = lambda i: (i, 0)
    return pl.pallas_call(
        _proj_rest_kernel,
        grid=(n // tm,),
        in_specs=[pl.BlockSpec((tm, D_MODEL), row),
                  pl.BlockSpec((1, D_MODEL), lambda i: (0, 0)),
                  _resident((D_MODEL, RWKV_COLS + 2 * D_MODEL), lambda i: (0, 0))],
        out_specs=[pl.BlockSpec((tm, RWKV_COLS), row),
                   pl.BlockSpec((tm, 2 * D_MODEL), row)],
        out_shape=[jax.ShapeDtypeStruct((n, RWKV_COLS), F32),
                   jax.ShapeDtypeStruct((n, 2 * D_MODEL), F32)],
        compiler_params=_cparams(1),
        name="proj_rest",
    )(x2d, g, w_rest)


def _shift_proj_kernel(s_ref, w_ref, o_ref):
    o_ref[...] = _dot(s_ref[...].astype(BF16), w_ref[...])


def _shift_proj(shift_rows, w_rw):
    b = shift_rows.shape[0]
    return pl.pallas_call(
        _shift_proj_kernel,
        out_shape=jax.ShapeDtypeStruct((b, RWKV_COLS), F32),
        compiler_params=pltpu.CompilerParams(vmem_limit_bytes=VMEM_LIMIT),
        name="shift_proj",
    )(shift_rows, w_rw)


def _bias_delta(btab, rb_ref, hd):
    out = jnp.zeros(btab.shape, F32)
    far = rb_ref[REL_BUCKETS - 1, hd]
    for j in range(REL_BUCKETS - 1):
        out = jnp.where(btab == j, rb_ref[j, hd] - far, out)
    return out


def _tri_ones():
    r = lax.broadcasted_iota(I32, (LANE, LANE), 0)
    c = lax.broadcasted_iota(I32, (LANE, LANE), 1)
    return jnp.where(r <= c, 1.0, 0.0).astype(BF16), jnp.ones((LANE, LANE), BF16)


def _prompt_attn_kernel(btab_ref, rb_ref, qt_ref, qit_ref, wt_ref, tailb_ref, kb_ref, vt_ref, o_ref,
                        sc, dt, qpad, qipad, m_s, l_s, acc_s, lg, *, topk):
    b = pl.program_id(0)
    i = pl.program_id(1)
    nc = (i * PQB + PQB - 1) // KC + 1

    @pl.when((b == 0) & (i == 0))
    def _():
        bt = btab_ref[...]
        for hd in range(N_HEADS_A):
            dt[hd] = _bias_delta(bt, rb_ref, hd)

    q_pos = lax.broadcasted_iota(I32, (KC, PQB), 1) + i * PQB
    k_off = lax.broadcasted_iota(I32, (KC, PQB), 0)

    def keys(ref, c, cols):
        return ref[pl.ds(pl.multiple_of(c * KC, KC), KC), cols]

    def fold8(x):
        rows = x.shape[0]
        while rows > 8:
            rows //= 2
            x = x[:rows, :] + x[rows:, :]
        return x

    pad_rows = lax.broadcasted_iota(I32, (LANE, PQB), 0) // HEAD_DIM
    qt = qt_ref[0]
    for hd in range(N_HEADS_A):
        pair = qt[(hd // 2) * LANE:(hd // 2 + 1) * LANE, :].astype(F32)
        qpad[hd] = jnp.where(pad_rows == hd % 2, pair, 0.0).astype(BF16)
    qit = qit_ref[0]
    zeros_idx = jnp.zeros((LANE - IDX_DIM, PQB), BF16)
    for hd in range(IDX_HEADS):
        qipad[hd] = jnp.concatenate([qit[hd * IDX_DIM:(hd + 1) * IDX_DIM, :], zeros_idx], axis=0)

    w_rows = [wt_ref[0, hd:hd + 1, :] * IDX_SCALE for hd in range(IDX_HEADS)]

    def score_chunk(c, carry):
        kc = keys(tailb_ref, c, slice(None))
        s = jnp.zeros((KC, PQB), F32)
        for hd in range(IDX_HEADS):
            s = s + jnp.maximum(_dot(kc, qipad[hd]), 0.0) * w_rows[hd]
        sc[c] = jnp.where(k_off + c * KC <= q_pos, s, NEG_INF)
        return carry

    lax.fori_loop(0, nc, score_chunk, 0)

    def count(cmp):
        def body(c, cnt):
            return cnt + fold8(jnp.where(cmp(sc[c]), 1.0, 0.0))
        return jnp.sum(lax.fori_loop(0, nc, body, jnp.zeros((8, PQB), F32)), axis=0, keepdims=True)

    thr = _kth_largest(lambda cand: count(lambda x: x >= cand), topk, (1, PQB))

    need = float(topk) - count(lambda x: x > thr)
    tri = jnp.where(lax.broadcasted_iota(I32, (KC, KC), 0) >= lax.broadcasted_iota(I32, (KC, KC), 1),
                    1.0, 0.0).astype(BF16)

    def mask_chunk(c, off):
        x = sc[c]
        eq = x == thr
        pre = off + _dot(tri, jnp.where(eq, 1.0, 0.0).astype(BF16))
        sel = ((x > thr) | (eq & (pre <= need))) & (k_off + c * KC <= q_pos)
        sc[c] = jnp.where(sel, 0.0, NEG_INF)
        return pre[KC - 1:KC, :]

    lax.fori_loop(0, nc, mask_chunk, jnp.zeros((1, PQB), F32))

    m_s[...] = jnp.full(m_s.shape, NEG_INF, F32)
    l_s[...] = jnp.zeros(l_s.shape, F32)
    acc_s[...] = jnp.zeros(acc_s.shape, F32)

    def attn_chunk(c, carry, near):
        for sb in range(KC // ATT_SUB):
            k0 = pl.multiple_of(c * KC + sb * ATT_SUB, ATT_SUB)
            mk = sc[c, sb * ATT_SUB:(sb + 1) * ATT_SUB, :]

            def logits(hd):
                pair = slice((hd // 2) * LANE, (hd // 2 + 1) * LANE)
                x = _dot(kb_ref[pl.ds(k0, ATT_SUB), pair], qpad[hd]) + mk
                if near:
                    parts = []
                    for j in range(ATT_SUB // PQB):
                        blk = (KC // PQB) * c + sb * (ATT_SUB // PQB) + j
                        delta = jnp.where(blk == i, dt[hd, PQB:, :], jnp.where(blk == i - 1, dt[hd, :PQB, :], 0.0))
                        parts.append(x[j * PQB:(j + 1) * PQB, :] + delta)
                    x = jnp.concatenate(parts, axis=0)
                return x

            m_fins, alphas = [], []
            for hd in range(N_HEADS_A):
                x = logits(hd)
                lg[hd] = x
                m_old = m_s[hd]
                m_new = jnp.maximum(m_old, jnp.max(x, axis=0, keepdims=True))
                m_fin = jnp.where(m_new == NEG_INF, 0.0, m_new)
                m_fins.append(m_fin)
                alphas.append(jnp.exp(m_old - m_fin))
                m_s[hd] = m_new
            ones_rows = jnp.ones((16, ATT_SUB), BF16)
            for hd in range(N_HEADS_A):
                p = jnp.exp(lg[hd] - m_fins[hd]).astype(BF16)
                vt_h = vt_ref[c, hd * HEAD_DIM:(hd + 1) * HEAD_DIM, sb * ATT_SUB:(sb + 1) * ATT_SUB]
                pv = _dot(jnp.concatenate([vt_h, ones_rows], axis=0), p)
                acc_s[hd] = alphas[hd] * acc_s[hd] + pv[:HEAD_DIM]
                l_s[hd] = alphas[hd] * l_s[hd] + pv[HEAD_DIM:HEAD_DIM + 1]
        return carry

    n_far = jnp.maximum(nc - 2, 0)
    lax.fori_loop(0, n_far, functools.partial(attn_chunk, near=False), 0)
    lax.fori_loop(n_far, nc, functools.partial(attn_chunk, near=True), 0)
    out_t = jnp.concatenate([acc_s[hd] / l_s[hd] for hd in range(N_HEADS_A)], axis=0)
    o_ref[...] = out_t.T


def _prompt_attn(btab, rel_bias, qt, qit, wt, tailb, kb, vt, batch, seq):
    nq = seq // PQB
    ncb = seq // KC
    per = KC // PQB
    topk = min(TOPK_MAX, seq // 4)
    qtile = lambda rows: pl.BlockSpec((1, rows, PQB), lambda b, i: (b * ncb + i // per, 0, i % per))
    return pl.pallas_call(
        functools.partial(_prompt_attn_kernel, topk=topk),
        grid=(batch, nq),
        in_specs=[pl.BlockSpec((2 * PQB, PQB), lambda b, i: (0, 0)),
                  pl.BlockSpec(memory_space=pltpu.SMEM),
                  qtile(A_WIDTH), qtile(IDX_HEADS * IDX_DIM), qtile(8),
                  _resident((seq, LANE), lambda b, i: (b, 0)),
                  _resident((seq, A_WIDTH), lambda b, i: (b, 0)),
                  _resident((ncb, A_WIDTH, KC), lambda b, i: (b, 0, 0))],
        out_specs=pl.BlockSpec((PQB, A_WIDTH), lambda b, i: (b * nq + i, 0)),
        out_shape=jax.ShapeDtypeStruct((batch * seq, A_WIDTH), F32),
        scratch_shapes=[pltpu.VMEM((ncb, KC, PQB), F32),
                        pltpu.VMEM((N_HEADS_A, 2 * PQB, PQB), F32),
                        pltpu.VMEM((N_HEADS_A, LANE, PQB), BF16),
                        pltpu.VMEM((IDX_HEADS, LANE, PQB), BF16),
                        pltpu.VMEM((N_HEADS_A, 1, PQB), F32),
                        pltpu.VMEM((N_HEADS_A, 1, PQB), F32),
                        pltpu.VMEM((N_HEADS_A, HEAD_DIM, PQB), F32),
                        pltpu.VMEM((N_HEADS_A, ATT_SUB, PQB), F32)],
        compiler_params=_cparams(2),
        name="prompt_attn",
    )(btab, rel_bias, qt, qit, wt, tailb, kb, vt)


SEL_PAGES = 16
ATT_PAGES = 8


def _sample_select_kernel(pt_ref, qi_ref, tail_ref, *rest, n_groups, t_new, topk):
    pages = rest[:SEL_PAGES]
    mpast_ref, mnew_ref, scv, scn = rest[SEL_PAGES:]
    g = pl.program_id(1)
    gw = SEL_PAGES * PAGE
    w4 = tail_ref[:, 64:68] * IDX_SCALE
    q4 = qi_ref[...].reshape(IDX_HEADS * t_new, IDX_DIM)

    def scores(d):
        s = jnp.zeros((t_new, d.shape[1]), F32)
        for hd in range(IDX_HEADS):
            s = s + jnp.maximum(d[hd * t_new:(hd + 1) * t_new, :], 0.0) * w4[:, hd:hd + 1]
        return s

    kpast_t = jnp.concatenate([p[0] for p in pages], axis=1).astype(BF16)
    scv[g] = scores(_dot(q4, kpast_t))

    @pl.when(g == n_groups - 1)
    def _():
        rown = lax.broadcasted_iota(I32, (t_new, LANE), 0)
        lanen = lax.broadcasted_iota(I32, (t_new, LANE), 1)
        knew = jnp.concatenate([tail_ref[:, 0:64], jnp.zeros((LANE - t_new, 64), F32)], axis=0).astype(BF16)
        sn = scores(_dot_t(q4, knew))
        causal_n = lanen <= rown
        scn[...] = jnp.where(causal_n, sn, NEG_INF)

        def count(cmp):
            cnt = jnp.zeros((t_new, LANE), F32)
            for gg in range(n_groups):
                hit = jnp.where(cmp(scv[gg]), 1.0, 0.0)
                for j in range(gw // LANE):
                    cnt = cnt + hit[:, j * LANE:(j + 1) * LANE]
            cnt = cnt + jnp.where(cmp(scn[...]), 1.0, 0.0)
            return jnp.sum(cnt, axis=-1, keepdims=True)

        thr = _kth_largest(lambda cand: count(lambda x: x >= cand), topk, (t_new, 1))
        need = float(topk) - count(lambda x: x > thr)
        tri, ones = _tri_ones()
        blocks = [(gg, j) for gg in range(n_groups) for j in range(gw // LANE)]
        tile = lambda gg, j: scv[gg, :, j * LANE:(j + 1) * LANE]
        ties = lambda x: jnp.where(x == thr, 1.0, 0.0).astype(BF16)
        off = jnp.zeros((t_new, LANE), F32)
        for gg, j in blocks:
            x = tile(gg, j)
            eqb = ties(x)
            sel = (x > thr) | ((x == thr) & (off + _dot(eqb, tri) <= need))
            mpast_ref[0, :, gg * gw + j * LANE:gg * gw + (j + 1) * LANE] = jnp.where(sel, 0.0, NEG_INF)
            off = off + jnp.sum(eqb.astype(F32), axis=-1, keepdims=True)
        x = scn[...]
        sel = ((x > thr) | ((x == thr) & (off + _dot(ties(x), tri) <= need))) & causal_n
        mnew_ref[0] = jnp.where(sel, 0.0, NEG_INF)


def _sample_select(page_table, qi_hm, tail, cache_idx_t, t_new):
    bsz, n_pages = page_table.shape
    n_groups = n_pages // SEL_PAGES
    past = n_pages * PAGE
    topk = min(TOPK_MAX, (past + t_new) // 4)

    def page_spec(j):
        return pl.BlockSpec((1, IDX_DIM, PAGE), lambda b, g, pt: (pt[b, g * SEL_PAGES + j], 0, 0))

    grid_spec = pltpu.PrefetchScalarGridSpec(
        num_scalar_prefetch=1,
        grid=(bsz, n_groups),
        in_specs=[pl.BlockSpec((IDX_HEADS, t_new, 64), lambda b, g, pt: (0, b, 0)),
                  pl.BlockSpec((t_new, 128), lambda b, g, pt: (b, 0))]
                 + [page_spec(j) for j in range(SEL_PAGES)],
        out_specs=[pl.BlockSpec((1, t_new, past), lambda b, g, pt: (b, 0, 0)),
                   pl.BlockSpec((1, t_new, LANE), lambda b, g, pt: (b, 0, 0))],
        scratch_shapes=[pltpu.VMEM((n_groups, t_new, SEL_PAGES * PAGE), F32),
                        pltpu.VMEM((t_new, LANE), F32)])
    return pl.pallas_call(
        functools.partial(_sample_select_kernel, n_groups=n_groups, t_new=t_new, topk=topk),
        grid_spec=grid_spec,
        out_shape=[jax.ShapeDtypeStruct((bsz, t_new, past), F32),
                   jax.ShapeDtypeStruct((bsz, t_new, LANE), F32)],
        compiler_params=_cparams(2),
        name="sample_select",
    )(page_table, qi_hm, tail, *([cache_idx_t] * SEL_PAGES))


def _sample_attn_kernel(pt_ref, btab_ref, rb_ref, q_ref, knew_ref, vnew_ref, mpast_ref, mnew_ref, *rest,
                        n_steps, t_new):
    kpages = rest[:ATT_PAGES]
    vpages = rest[ATT_PAGES:2 * ATT_PAGES]
    o_ref, qbd, dq, m_s, l_s, acc_s = rest[2 * ATT_PAGES:]
    g = pl.program_id(1)
    hq = N_HEADS_A * t_new
    gw = ATT_PAGES * PAGE
    head_of_row = lax.broadcasted_iota(I32, (hq, A_WIDTH), 0) // t_new
    head_of_col = lax.broadcasted_iota(I32, (hq, A_WIDTH), 1) // HEAD_DIM
    own = head_of_row == head_of_col

    @pl.when((pl.program_id(0) == 0) & (g == 0))
    def _():
        bt = btab_ref[0:t_new, :]
        for hd in range(N_HEADS_A):
            dq[hd * t_new:(hd + 1) * t_new, :] = _bias_delta(bt, rb_ref, hd)

    @pl.when(g == 0)
    def _():
        qt = jnp.concatenate([q_ref[...].astype(F32)] * N_HEADS_A, axis=0)
        qbd[...] = jnp.where(own, qt, 0.0).astype(BF16)
        m_s[...] = jnp.full((hq, 1), NEG_INF, F32)
        l_s[...] = jnp.zeros((hq, 1), F32)
        acc_s[...] = jnp.zeros((hq, A_WIDTH), F32)

    def online(x, vb_t):
        m_old = m_s[...]
        m_new = jnp.maximum(m_old, jnp.max(x, axis=-1, keepdims=True))
        m_fin = jnp.where(m_new == NEG_INF, 0.0, m_new)
        alpha = jnp.exp(m_old - m_fin)
        p = jnp.exp(x - m_fin)
        l_s[...] = alpha * l_s[...] + jnp.sum(p, axis=-1, keepdims=True)
        acc_s[...] = alpha * acc_s[...] + _dot_t(p.astype(BF16), vb_t)
        m_s[...] = m_new

    kb_t = jnp.concatenate([p[0] for p in kpages], axis=1).astype(BF16)
    vb_t = jnp.concatenate([p[0] for p in vpages], axis=1).astype(BF16)
    x = _dot(qbd[...], kb_t) + jnp.concatenate([mpast_ref[0]] * N_HEADS_A, axis=0)
    lane = lax.broadcasted_iota(I32, (hq, gw), 1)
    near = jnp.concatenate([jnp.zeros((hq, gw - LANE), F32), dq[:, :LANE]], axis=1)
    x = x + jnp.where(g == n_steps - 1, near, 0.0)

    online(x, vb_t)

    @pl.when(g == n_steps - 1)
    def _():
        pad = jnp.zeros((LANE - t_new, A_WIDTH), F32)
        kn = jnp.concatenate([knew_ref[...], pad], axis=0).astype(BF16)
        vn = jnp.concatenate([vnew_ref[...], pad], axis=0)
        xn = _dot_t(qbd[...], kn) + jnp.concatenate([mnew_ref[0]] * N_HEADS_A, axis=0) + dq[:, LANE:]
        online(xn, vn.T.astype(BF16))
        res = jnp.where(own, acc_s[...] / l_s[...], 0.0)
        out = res[0:t_new, :]
        for hd in range(1, N_HEADS_A):
            out = out + res[hd * t_new:(hd + 1) * t_new, :]
        o_ref[...] = out


def _sample_attn(page_table, btab, rel_bias, q_rows, k_new, v_new, mpast, mnew, cache_kt, cache_vt, t_new):
    bsz, n_pages = page_table.shape
    n_steps = n_pages // ATT_PAGES
    hq = N_HEADS_A * t_new

    def page_spec(j):
        return pl.BlockSpec((1, A_WIDTH, PAGE), lambda b, g, pt: (pt[b, g * ATT_PAGES + j], 0, 0))

    grid_spec = pltpu.PrefetchScalarGridSpec(
        num_scalar_prefetch=1,
        grid=(bsz, n_steps),
        in_specs=[pl.BlockSpec((QB, 2 * QB), lambda b, g, pt: (0, 0)),
                  pl.BlockSpec(memory_space=pltpu.SMEM),
                  pl.BlockSpec((t_new, A_WIDTH), lambda b, g, pt: (b, 0)),
                  pl.BlockSpec((t_new, A_WIDTH), lambda b, g, pt: (b, 0)),
                  pl.BlockSpec((t_new, A_WIDTH), lambda b, g, pt: (b, 0)),
                  pl.BlockSpec((1, t_new, ATT_PAGES * PAGE), lambda b, g, pt: (b, 0, g)),
                  pl.BlockSpec((1, t_new, LANE), lambda b, g, pt: (b, 0, 0))]
                 + [page_spec(j) for j in range(ATT_PAGES)] * 2,
        out_specs=pl.BlockSpec((t_new, A_WIDTH), lambda b, g, pt: (b, 0)),
        scratch_shapes=[pltpu.VMEM((hq, A_WIDTH), BF16),
                        pltpu.VMEM((hq, 2 * QB), F32),
                        pltpu.VMEM((hq, 1), F32),
                        pltpu.VMEM((hq, 1), F32),
                        pltpu.VMEM((hq, A_WIDTH), F32)])
    return pl.pallas_call(
        functools.partial(_sample_attn_kernel, n_steps=n_steps, t_new=t_new),
        grid_spec=grid_spec,
        out_shape=jax.ShapeDtypeStruct((bsz * t_new, A_WIDTH), F32),
        compiler_params=_cparams(2),
        name="sample_attn",
    )(page_table, btab, rel_bias, q_rows, k_new, v_new, mpast, mnew,
      *([cache_kt] * ATT_PAGES), *([cache_vt] * ATT_PAGES))


def _head_block_ones():
    r = lax.broadcasted_iota(I32, (B_WIDTH, B_WIDTH), 0) // HEAD_B
    c = lax.broadcasted_iota(I32, (B_WIDTH, B_WIDTH), 1) // HEAD_B
    return jnp.where(r == c, 1.0, 0.0).astype(BF16)


def _rwkv_prep_kernel(rw_ref, prev8_ref, prow_ref, mu_ref, w0_ref, a0_ref, kk_ref, ka_ref, rk_ref,
                      w2_ref, a2_ref, g2_ref,
                      r_ref, w_ref, k_ref, v_ref, na_ref, bb_ref, bonus_ref, g_ref, *, tt):
    t = pl.program_id(1)
    cur = rw_ref[0]
    first_prev = jnp.where(t == 0, prow_ref[0], prev8_ref[0, 7:8, :])
    rows = lax.broadcasted_iota(I32, cur.shape, 0)
    prev = jnp.where(rows == 0, first_prev, pltpu.roll(cur, 1, 0))
    mixed = cur + (prev - cur) * mu_ref[...]
    r = mixed[:, 0:512]
    k = mixed[:, 512:1024]
    v = mixed[:, 1024:1536]
    wd = mixed[:, 1536:1600]
    ad = mixed[:, 1600:1664]
    gd = mixed[:, 1664:1792]
    z = w0_ref[...] + _dot(jnp.tanh(wd).astype(BF16), w2_ref[...])
    u = -z
    softplus = jnp.maximum(u, 0.0) + jnp.log1p(jnp.exp(-jnp.abs(u)))
    decay = jnp.exp(-jnp.exp(-softplus - 0.5))
    a = jax.nn.sigmoid(a0_ref[...] + _dot(ad.astype(BF16), a2_ref[...]))
    g = _dot(jax.nn.sigmoid(gd).astype(BF16), g2_ref[...])
    bd = _head_block_ones()
    kk = k * kk_ref[...]
    nrm = jnp.sqrt(_split_dot(kk * kk, bd))
    kk = kk / jnp.maximum(nrm, 1e-12)
    k2 = k * (1.0 + (a - 1.0) * ka_ref[...])
    r_ref[0] = r
    w_ref[0] = decay
    k_ref[0] = k2
    v_ref[0] = v
    na_ref[0] = -kk
    bb_ref[0] = kk * a
    bonus_ref[0] = _split_dot(r * k2 * rk_ref[...], bd) * v
    g_ref[0] = g


def _rwkv_prep(rw3, prow, p, tt):
    bsz, t_len, _ = rw3.shape
    vec = lambda c: pl.BlockSpec((1, c), lambda b, t: (0, 0))
    full = lambda r, c: pl.BlockSpec((r, c), lambda b, t: (0, 0))
    tile = pl.BlockSpec((1, tt, B_WIDTH), lambda b, t: (b, t, 0))
    return pl.pallas_call(
        functools.partial(_rwkv_prep_kernel, tt=tt),
        grid=(bsz, t_len // tt),
        in_specs=[pl.BlockSpec((1, tt, RWKV_COLS), lambda b, t: (b, t, 0)),
                  pl.BlockSpec((1, 8, RWKV_COLS), lambda b, t: (b, jnp.maximum(t * (tt // 8) - 1, 0), 0)),
                  pl.BlockSpec((1, 1, RWKV_COLS), lambda b, t: (b, 0, 0)),
                  vec(RWKV_COLS), vec(B_WIDTH), vec(B_WIDTH), vec(B_WIDTH), vec(B_WIDTH), vec(B_WIDTH),
                  full(D_DECAY, B_WIDTH), full(D_AAA, B_WIDTH), full(D_GATE, B_WIDTH)],
        out_specs=[tile] * 8,
        out_shape=[jax.ShapeDtypeStruct((bsz, t_len, B_WIDTH), F32)] * 8,
        compiler_params=_cparams(2),
        name="rwkv_prep",
    )(rw3, rw3, prow.reshape(bsz, 1, RWKV_COLS), p["mu"], p["w0"], p["a0"], p["k_k"], p["k_a"], p["r_k"],
      p["w2"], p["a2"], p["g2"])


SCAN_GROUP = 8


def _rwkv_scan_kernel(r_ref, w_ref, k_ref, v_ref, na_ref, bb_ref, s0_ref, y_ref, st_ref, *, bb_n, tc):
    @pl.when(pl.program_id(1) == 0)
    def _():
        st_ref[...] = s0_ref[...]

    lane = lax.broadcasted_iota(I32, (HEAD_B, LANE), 1)
    rowi = lax.broadcasted_iota(I32, (HEAD_B, LANE), 0)
    diag = (lane & (HEAD_B - 1)) == rowi
    step_lane = lane & (HEAD_B - 1)
    bj = lax.broadcasted_iota(I32, (2 * LANE, LANE), 0)
    bl = lax.broadcasted_iota(I32, (2 * LANE, LANE), 1)
    bd2 = jnp.where((bj & (LANE - 1)) // HEAD_B == bl // HEAD_B, 1.0, 0.0).astype(BF16)
    bd1 = bd2[0:LANE]

    def hi_lo(x):
        hi = x.astype(BF16)
        return hi, (x - hi.astype(F32)).astype(BF16)

    chains = [(bi, pr) for bi in range(bb_n) for pr in range(N_HEADS_B // 2)]
    groups = [chains[g:g + SCAN_GROUP] for g in range(0, len(chains), SCAN_GROUP)]

    def tile(t8, carry):
        t0 = pl.multiple_of(t8 * 8, 8)
        ins = {c: tuple(ref[c[0], pl.ds(t0, 8), c[1] * LANE:(c[1] + 1) * LANE]
                        for ref in (r_ref, w_ref, k_ref, v_ref, na_ref, bb_ref)) for c in chains}
        state = {c: st_ref[c[0], c[1]] for c in chains}
        ycols = {c: jnp.zeros((HEAD_B, LANE), F32) for c in chains}
        for i in range(8):
            for grp in groups:
                rows = {c: tuple(x[i:i + 1, :] for x in ins[c]) for c in grp}
                p_all, z_all = [], []
                for c in grp:
                    rr, ww, kk, vv, aa, bb = rows[c]
                    p_all.append(jnp.concatenate(hi_lo(state[c] * aa), axis=1))
                    z_all.append(jnp.where(diag, vv, 0.0).astype(BF16))
                sab_all = _dot(jnp.concatenate(p_all, axis=0), bd2)
                vc_all = _dot(jnp.concatenate(z_all, axis=0), bd1)
                y_all = []
                for j, c in enumerate(grp):
                    rr, ww, kk, vv, aa, bb = rows[c]
                    js = slice(j * HEAD_B, (j + 1) * HEAD_B)
                    s = state[c] * ww + sab_all[js] * bb + vc_all[js] * kk
                    state[c] = s
                    y_all.append((s * rr).astype(BF16))
                yb_all = _dot(jnp.concatenate(y_all, axis=0), bd1)
                for j, c in enumerate(grp):
                    ycols[c] = jnp.where(step_lane == i, yb_all[j * HEAD_B:(j + 1) * HEAD_B], ycols[c])
        for c in chains:
            st_ref[c[0], c[1]] = state[c]
            yt = ycols[c].T
            y_ref[c[0], pl.ds(t0, 8), c[1] * LANE:c[1] * LANE + HEAD_B] = yt[0:8]
            y_ref[c[0], pl.ds(t0, 8), c[1] * LANE + HEAD_B:(c[1] + 1) * LANE] = yt[HEAD_B:HEAD_B + 8]
        return carry

    lax.fori_loop(0, tc // 8, tile, 0)


def _pair_pack(state):
    b = state.shape[0]
    s = state.reshape(b, N_HEADS_B // 2, 2, HEAD_B, HEAD_B)
    return jnp.transpose(s, (0, 1, 3, 2, 4)).reshape(b, N_HEADS_B // 2, HEAD_B, 2 * HEAD_B)


def _pair_unpack(state):
    b = state.shape[0]
    s = state.reshape(b, N_HEADS_B // 2, HEAD_B, 2, HEAD_B)
    return jnp.transpose(s, (0, 1, 3, 2, 4)).reshape(b, N_HEADS_B, HEAD_B, HEAD_B)


def _rwkv_scan(r, w, k, v, na, bb, state0, bb_n, tc):
    bsz, t_len, _ = r.shape
    tile = pl.BlockSpec((bb_n, tc, B_WIDTH), lambda b, t: (b, t, 0))
    st = pl.BlockSpec((bb_n, N_HEADS_B // 2, HEAD_B, 2 * HEAD_B), lambda b, t: (b, 0, 0, 0))
    y, state = pl.pallas_call(
        functools.partial(_rwkv_scan_kernel, bb_n=bb_n, tc=tc),
        grid=(bsz // bb_n, t_len // tc),
        in_specs=[tile] * 6 + [st],
        out_specs=[tile, st],
        out_shape=[jax.ShapeDtypeStruct((bsz, t_len, B_WIDTH), F32),
                   jax.ShapeDtypeStruct((bsz, N_HEADS_B // 2, HEAD_B, 2 * HEAD_B), F32)],
        compiler_params=_cparams(2),
        name="rwkv_scan",
    )(r, w, k, v, na, bb, _pair_pack(state0))
    return y, _pair_unpack(state)


def _tail_kernel(x_ref, att_ref, y_ref, bonus_ref, g_ref, gate_ref, lnw_ref, lnb_ref, nm_ref, nf_ref,
                 wa_ref, wb_ref, wo_ref, w1_ref, w2_ref, o_ref):
    bd = _head_block_ones()
    y = y_ref[...]
    mean = _split_dot(y, bd) * (1.0 / HEAD_B)
    yc = y - mean
    var = _split_dot(yc * yc, bd) * (1.0 / HEAD_B)
    yn = yc * lax.rsqrt(var + GN_EPS) * lnw_ref[...] + lnb_ref[...]
    rwkv = (yn + bonus_ref[...]) * g_ref[...]
    ga = gate_ref[:, 0:D_MODEL]
    gb = gate_ref[:, D_MODEL:2 * D_MODEL]
    merged = (jax.nn.sigmoid(ga) * _dot(att_ref[...].astype(BF16), wa_ref[...])
              + jax.nn.sigmoid(gb) * _dot(rwkv.astype(BF16), wb_ref[...]))
    x1 = x_ref[...] + _dot(merged.astype(BF16), wo_ref[...])
    hm = _rms(x1, nm_ref[...]).astype(BF16)
    up = jnp.maximum(_dot(hm, w1_ref[...]), 0.0)
    x2 = x1 + _dot((up * up).astype(BF16), w2_ref[...])
    o_ref[...] = _rms(x2, nf_ref[...])


def _tail(x2d, att, y, bonus, g, gates, p, tm):
    n = x2d.shape[0]
    row = lambda i: (i, 0)
    vec = lambda c: pl.BlockSpec((1, c), lambda i: (0, 0))
    res = lambda r, c: _resident((r, c), lambda i: (0, 0))
    return pl.pallas_call(
        _tail_kernel,
        grid=(n // tm,),
        in_specs=[pl.BlockSpec((tm, D_MODEL), row),
                  pl.BlockSpec((tm, A_WIDTH), row), pl.BlockSpec((tm, B_WIDTH), row),
                  pl.BlockSpec((tm, B_WIDTH), row), pl.BlockSpec((tm, B_WIDTH), row),
                  pl.BlockSpec((tm, 2 * D_MODEL), row),
                  vec(B_WIDTH), vec(B_WIDTH), vec(D_MODEL), vec(D_MODEL),
                  res(A_WIDTH, D_MODEL), res(B_WIDTH, D_MODEL), res(D_MODEL, D_MODEL),
                  res(D_MODEL, D_FF), res(D_FF, D_MODEL)],
        out_specs=pl.BlockSpec((tm, D_MODEL), row),
        out_shape=jax.ShapeDtypeStruct((n, D_MODEL), F32),
        compiler_params=_cparams(1),
        name="tail",
    )(x2d, att, y, bonus, g, gates, p["ln_w"], p["ln_b"], p["norm_mlp"], p["norm_final"],
      p["wa"], p["wb"], p["wo"], p["w1"], p["w2m"])


def _prepare_params(norm_mix, w_in, rwkv_mu, rwkv_w0, rwkv_w2, rwkv_a0, rwkv_a2, rwkv_g2, rwkv_k_k, rwkv_k_a,
                    rwkv_r_k, rwkv_ln_w, rwkv_ln_b, w_branch_a, w_branch_b, w_out, norm_mlp, w_mlp_in,
                    w_mlp_out, norm_final):
    wb16 = w_in.astype(BF16)
    o_idx = 3 * A_WIDTH
    o_kidx = o_idx + IDX_HEADS * IDX_DIM
    o_widx = o_kidx + IDX_DIM
    pad = jnp.zeros((D_MODEL, 128 - IDX_DIM - IDX_HEADS), BF16)
    w_attn = jnp.concatenate([wb16[:, :o_kidx], wb16[:, o_kidx:o_widx + IDX_HEADS], pad], axis=1)
    o = RWKV_OFF
    def regroup(a):
        return jnp.concatenate([a[..., 0:512], a[..., 576:1600], a[..., 512:576], a[..., 1600:1792]], axis=-1)
    w_rw = regroup(wb16[:, o:o + RWKV_COLS])
    w_rest = jnp.concatenate([w_rw, wb16[:, o + RWKV_COLS:]], axis=1)
    r1 = lambda a: a.reshape(1, -1).astype(F32)
    return dict(
        norm_mix=r1(norm_mix), w_attn=w_attn, w_rest=w_rest, w_rw=w_rw,
        mu=r1(regroup(rwkv_mu)), w0=r1(rwkv_w0), a0=r1(rwkv_a0), k_k=r1(rwkv_k_k), k_a=r1(rwkv_k_a),
        r_k=r1(rwkv_r_k), w2=rwkv_w2.astype(BF16), a2=rwkv_a2.astype(BF16), g2=rwkv_g2.astype(BF16),
        ln_w=r1(rwkv_ln_w), ln_b=r1(rwkv_ln_b), norm_mlp=r1(norm_mlp), norm_final=r1(norm_final),
        wa=w_branch_a.astype(BF16), wb=w_branch_b.astype(BF16), wo=w_out.astype(BF16),
        w1=w_mlp_in.astype(BF16), w2m=w_mlp_out.astype(BF16))


def _layer(x, shift_rows, wkv0, attend, p, tm, tm_tail, scan_tc):
    bsz, t_len, _ = x.shape
    n = bsz * t_len
    x2d = x.reshape(n, D_MODEL)
    hl_rows = 8 if t_len >= tm else tm
    names = ("q_hm", "k", "v", "qt", "kb", "vt", "qi_hm", "qit", "tail", "tailb", "wt", "hlast")
    pr = dict(zip(names, _proj_attn(x2d, p["norm_mix"], p["w_attn"], tm, hl_rows)))
    k, v, tail, hlast = pr["k"], pr["v"], pr["tail"], pr["hlast"]
    rw, gates = _proj_rest(x2d, p["norm_mix"], p["w_rest"], tm)
    att = attend(pr)
    prow = _shift_proj(shift_rows, p["w_rw"])
    r, w, k2, v2, na, bb, bonus, g = _rwkv_prep(rw.reshape(bsz, t_len, RWKV_COLS), prow, p, min(256, t_len))
    y, wkv = _rwkv_scan(r, w, k2, v2, na, bb, wkv0, 2, scan_tc)
    flat = lambda a: a.reshape(n, B_WIDTH)
    out = _tail(x2d, att, flat(y), flat(bonus), flat(g), gates, p, tm_tail)
    if t_len >= tm:
        shift = hlast.reshape(bsz, t_len // tm, 8, D_MODEL)[:, -1, 7]
    else:
        shift = hlast.reshape(bsz, t_len, D_MODEL)[:, -1]
    return (out.reshape(bsz, t_len, D_MODEL), k.reshape(bsz, t_len, N_HEADS_A, HEAD_DIM),
            v.reshape(bsz, t_len, N_HEADS_A, HEAD_DIM), tail[:, :IDX_DIM].reshape(bsz, t_len, IDX_DIM), wkv, shift)


def kernel(x_prompt, x_sample, cache_k, cache_v, cache_idx_k, state_wkv, state_shift, page_table, rel_bias, norm_mix, w_in, rwkv_mu, rwkv_w0, rwkv_w2, rwkv_a0, rwkv_a2, rwkv_g2, rwkv_k_k, rwkv_k_a, rwkv_r_k, rwkv_ln_w, rwkv_ln_b, w_branch_a, w_branch_b, w_out, norm_mlp, w_mlp_in, w_mlp_out, norm_final):
    p = _prepare_params(norm_mix, w_in, rwkv_mu, rwkv_w0, rwkv_w2, rwkv_a0, rwkv_a2, rwkv_g2, rwkv_k_k,
                        rwkv_k_a, rwkv_r_k, rwkv_ln_w, rwkv_ln_b, w_branch_a, w_branch_b, w_out, norm_mlp,
                        w_mlp_in, w_mlp_out, norm_final)
    btab = jnp.asarray(_window_buckets())
    btab_t = jnp.asarray(_window_buckets_t())
    b_p, s_p, _ = x_prompt.shape
    b_s, t_s, _ = x_sample.shape
    n_pool = cache_k.shape[0]

    def prompt_attend(pr):
        return _prompt_attn(btab_t, rel_bias, pr["qt"], pr["qit"], pr["wt"], pr["tailb"], pr["kb"], pr["vt"],
                            b_p, s_p)

    def sample_attend(pr):
        key_minor = lambda c: jnp.transpose(c.reshape(n_pool, PAGE, -1), (0, 2, 1))
        mpast, mnew = _sample_select(page_table, pr["qi_hm"], pr["tail"], key_minor(cache_idx_k), t_s)
        q_rows = jnp.transpose(pr["q_hm"], (1, 0, 2)).reshape(b_s * t_s, A_WIDTH)
        return _sample_attn(page_table, btab, rel_bias, q_rows, pr["k"], pr["v"], mpast, mnew,
                            key_minor(cache_k), key_minor(cache_v), t_s)

    zero_shift = jnp.zeros((b_p, D_MODEL), F32)
    zero_wkv = jnp.zeros((b_p, N_HEADS_B, HEAD_B, HEAD_B), F32)
    y_p, k_p, v_p, ik_p, wkv_p, sh_p = _layer(x_prompt, zero_shift, zero_wkv, prompt_attend, p, KC, 256, 256)
    y_s, k_s, v_s, ik_s, wkv_s, sh_s = _layer(x_sample, state_shift, state_wkv, sample_attend, p,
                                              b_s * t_s, b_s * t_s, t_s)
    return (y_p, y_s, k_p, v_p, ik_p, wkv_p, sh_p, k_s, v_s, ik_s, wkv_s, sh_s)
```

```python
import functools
import math

import numpy as np
import jax
import jax.numpy as jnp
from jax import lax
from jax.experimental import pallas as pl
from jax.experimental.pallas import tpu as pltpu

F32 = jnp.float32
BF16 = jnp.bfloat16
I32 = jnp.int32

D_MODEL = 1024
PAGE = 128
HEAD_DIM = 64
A_WIDTH = 512
N_HEADS_A = 8
IDX_HEADS = 4
IDX_DIM = 64
IDX_SCALE = (IDX_HEADS * IDX_DIM) ** -0.5
ATT_SCALE = HEAD_DIM ** -0.5
TOPK_MAX = 256
REL_BUCKETS = 32
REL_MAX_DIST = 128
HEAD_B = 64
B_WIDTH = 512
N_HEADS_B = 8
D_DECAY = 64
D_AAA = 64
D_GATE = 128
GN_EPS = 64e-5
D_FF = 4096
RMS_EPS = 1e-6
RWKV_COLS = 3 * B_WIDTH + D_DECAY + D_AAA + D_GATE
RWKV_OFF = 3 * A_WIDTH + IDX_HEADS * IDX_DIM + IDX_DIM + IDX_HEADS

LANE = 128
QB = 128
PQB = 256
KC = 512
ATT_SUB = KC
NEG_INF = float("-inf")
INT_MIN = -(2 ** 31)
VMEM_LIMIT = 56 * 1024 * 1024


def _cparams(n_axes):
    return pltpu.CompilerParams(dimension_semantics=("arbitrary",) * n_axes,
                                vmem_limit_bytes=VMEM_LIMIT)


def _resident(shape, index_map):
    return pl.BlockSpec(shape, index_map, pipeline_mode=pl.Buffered(1))


def _bucket_table(dist):
    dist = np.asarray(dist, np.int64)
    max_exact = REL_BUCKETS // 2
    d_f = np.maximum(dist, max_exact).astype(np.float32)
    large = max_exact + (np.log(d_f / np.float32(max_exact)) / np.float32(math.log(REL_MAX_DIST / max_exact))
                         * np.float32(REL_BUCKETS - max_exact)).astype(np.int32)
    large = np.minimum(large, REL_BUCKETS - 1)
    return np.where(dist < max_exact, dist, large).astype(np.int32)


def _window_buckets():
    t = np.arange(QB)[:, None]
    c = np.arange(2 * QB)[None, :]
    return _bucket_table(np.maximum(QB + t - c, 0))


def _window_buckets_t():
    k = np.arange(2 * PQB)[:, None]
    q = np.arange(PQB)[None, :]
    return _bucket_table(np.maximum(PQB + q - k, 0))


def _rms(x, g):
    return x * lax.rsqrt(jnp.mean(x * x, axis=-1, keepdims=True) + RMS_EPS) * g


def _dot(a, b):
    return jnp.dot(a, b, preferred_element_type=F32)


def _dot_t(a, b):
    return lax.dot_general(a, b, (((1,), (1,)), ((), ())), preferred_element_type=F32)


def _split_dot(x, m_bf16):
    hi = x.astype(BF16)
    lo = (x - hi.astype(F32)).astype(BF16)
    return _dot(hi, m_bf16) + _dot(lo, m_bf16)


def _kth_largest(count_ge, k, shape):
    def bit_step(it, t_u):
        cand_u = t_u | jnp.left_shift(jnp.int32(1), 31 - it)
        return jnp.where(count_ge(_ordered_f32(cand_u)) >= float(k), cand_u, t_u)
    return _ordered_f32(lax.fori_loop(0, 32, bit_step, jnp.zeros(shape, I32)))


def _ordered_f32(u):
    s = u ^ INT_MIN
    bits = jnp.where(s < 0, INT_MIN | (-s), s)
    return pltpu.bitcast(bits, F32)


def _proj_attn_kernel(x_ref, g_ref, w_ref, q_ref, k_ref, v_ref, qt_ref, kb_ref, vt_ref, qi_ref, qit_ref,
                      tail_ref, tailb_ref, wt_ref, hl_ref, *, tm, hl_rows, kv_feature_major):
    h = _rms(x_ref[...], g_ref[...])
    hl_ref[...] = h[tm - hl_rows:, :]
    hb = h.astype(BF16)
    q = _dot(hb, w_ref[:, 0:512]) * ATT_SCALE
    for hd in range(N_HEADS_A):
        q_ref[hd] = q[:, hd * 64:(hd + 1) * 64].astype(BF16)
    qt_ref[0] = q.T.astype(BF16)
    k = _dot(hb, w_ref[:, 512:1024])
    kb_ref[...] = k.astype(BF16)
    v = _dot(hb, w_ref[:, 1024:1536])
    v_t = v.T
    vt_ref[0] = v_t.astype(BF16)
    if kv_feature_major:
        k_ref[0] = k.T
        v_ref[0] = v_t
    else:
        k_ref[...] = k
        v_ref[...] = v
    qi = _dot(hb, w_ref[:, 1536:1792])
    for hd in range(IDX_HEADS):
        qi_ref[hd] = qi[:, hd * 64:(hd + 1) * 64].astype(BF16)
    qit_ref[0] = qi.T.astype(BF16)
    tail = _dot(hb, w_ref[:, 1792:1920])
    tail_ref[...] = tail
    tailb_ref[...] = tail.astype(BF16)
    wt_ref[0] = tail.T[IDX_DIM:IDX_DIM + 8, :]


def _proj_attn(x2d, g, w_attn, tm, hl_rows, seq_tiles):
    n = x2d.shape[0]
    nt = n // tm
    row = lambda i: (i, 0)
    if seq_tiles:
        kv_spec = pl.BlockSpec((1, 512, tm), lambda i: (i // seq_tiles, 0, i % seq_tiles))
        kv_shape = jax.ShapeDtypeStruct((nt // seq_tiles, 512, seq_tiles * tm), F32)
    else:
        kv_spec = pl.BlockSpec((tm, 512), row)
        kv_shape = jax.ShapeDtypeStruct((n, 512), F32)
    return pl.pallas_call(
        functools.partial(_proj_attn_kernel, tm=tm, hl_rows=hl_rows, kv_feature_major=bool(seq_tiles)),
        grid=(nt,),
        in_specs=[pl.BlockSpec((tm, D_MODEL), row),
                  pl.BlockSpec((1, D_MODEL), lambda i: (0, 0)),
                  _resident((D_MODEL, 1920), lambda i: (0, 0))],
        out_specs=[pl.BlockSpec((N_HEADS_A, tm, 64), lambda i: (0, i, 0)),
                   kv_spec,
                   kv_spec,
                   pl.BlockSpec((1, 512, tm), lambda i: (i, 0, 0)),
                   pl.BlockSpec((tm, 512), row),
                   pl.BlockSpec((1, 512, tm), lambda i: (i, 0, 0)),
                   pl.BlockSpec((IDX_HEADS, tm, 64), lambda i: (0, i, 0)),
                   pl.BlockSpec((1, 256, tm), lambda i: (i, 0, 0)),
                   pl.BlockSpec((tm, 128), row),
                   pl.BlockSpec((tm, 128), row),
                   pl.BlockSpec((1, 8, tm), lambda i: (i, 0, 0)),
                   pl.BlockSpec((hl_rows, D_MODEL), row)],
        out_shape=[jax.ShapeDtypeStruct((N_HEADS_A, n, 64), BF16),
                   kv_shape,
                   kv_shape,
                   jax.ShapeDtypeStruct((nt, 512, tm), BF16),
                   jax.ShapeDtypeStruct((n, 512), BF16),
                   jax.ShapeDtypeStruct((nt, 512, tm), BF16),
                   jax.ShapeDtypeStruct((IDX_HEADS, n, 64), BF16),
                   jax.ShapeDtypeStruct((nt, 256, tm), BF16),
                   jax.ShapeDtypeStruct((n, 128), F32),
                   jax.ShapeDtypeStruct((n, 128), BF16),
                   jax.ShapeDtypeStruct((nt, 8, tm), F32),
                   jax.ShapeDtypeStruct((nt * hl_rows, D_MODEL), F32)],
        compiler_params=_cparams(1),
        name="proj_attn",
    )(x2d, g, w_attn)


def _proj_rest_kernel(x_ref, g_ref, w_ref, rw_ref, gate_ref):
    hb = _rms(x_ref[...], g_ref[...]).astype(BF16)
    rw_ref[...] = _dot(hb, w_ref[:, 0:RWKV_COLS])
    gate_ref[...] = _dot(hb, w_ref[:, RWKV_COLS:RWKV_COLS + 2 * D_MODEL])


def _proj_rest(x2d, g, w_rest, tm):
    n = x2d.shape[0]
    row = lambda i: (i, 0)
    return pl.pallas_call(
        _proj_rest_kernel,
        grid=(n // tm,),
        in_specs=[pl.BlockSpec((tm, D_MODEL), row),
                  pl.BlockSpec((1, D_MODEL), lambda i: (0, 0)),
                  _resident((D_MODEL, RWKV_COLS + 2 * D_MODEL), lambda i: (0, 0))],
        out_specs=[pl.BlockSpec((tm, RWKV_COLS), row),
                   pl.BlockSpec((tm, 2 * D_MODEL), row)],
        out_shape=[jax.ShapeDtypeStruct((n, RWKV_COLS), F32),
                   jax.ShapeDtypeStruct((n, 2 * D_MODEL), F32)],
        compiler_params=_cparams(1),
        name="proj_rest",
    )(x2d, g, w_rest)


def _shift_proj_kernel(s_ref, w_ref, o_ref):
    o_ref[...] = _dot(s_ref[...].astype(BF16), w_ref[...])


def _shift_proj(shift_rows, w_rw):
    b = shift_rows.shape[0]
    return pl.pallas_call(
        _shift_proj_kernel,
        out_shape=jax.ShapeDtypeStruct((b, RWKV_COLS), F32),
        compiler_params=pltpu.CompilerParams(vmem_limit_bytes=VMEM_LIMIT),
        name="shift_proj",
    )(shift_rows, w_rw)


def _bias_delta(btab, rb_ref, hd):
    out = jnp.zeros(btab.shape, F32)
    far = rb_ref[REL_BUCKETS - 1, hd]
    for j in range(REL_BUCKETS - 1):
        out = jnp.where(btab == j, rb_ref[j, hd] - far, out)
    return out


def _tri_ones():
    r = lax.broadcasted_iota(I32, (LANE, LANE), 0)
    c = lax.broadcasted_iota(I32, (LANE, LANE), 1)
    return jnp.where(r <= c, 1.0, 0.0).astype(BF16), jnp.ones((LANE, LANE), BF16)


def _prompt_attn_kernel(btab_ref, rb_ref, qt_ref, qit_ref, wt_ref, tailb_ref, kb_ref, vt_ref, o_ref,
                        sc, dt, qpad, qipad, m_s, l_s, acc_s, lg, *, topk):
    b = pl.program_id(0)
    i = pl.program_id(1)
    nc = (i * PQB + PQB - 1) // KC + 1

    @pl.when((b == 0) & (i == 0))
    def _():
        bt = btab_ref[...]
        for hd in range(N_HEADS_A):
            dt[hd] = _bias_delta(bt, rb_ref, hd)

    q_pos = lax.broadcasted_iota(I32, (KC, PQB), 1) + i * PQB
    k_off = lax.broadcasted_iota(I32, (KC, PQB), 0)

    def keys(ref, c, cols):
        return ref[pl.ds(pl.multiple_of(c * KC, KC), KC), cols]

    def fold8(x):
        rows = x.shape[0]
        while rows > 8:
            rows //= 2
            x = x[:rows, :] + x[rows:, :]
        return x

    pad_rows = lax.broadcasted_iota(I32, (LANE, PQB), 0) // HEAD_DIM
    qt = qt_ref[0]
    for hd in range(N_HEADS_A):
        pair = qt[(hd // 2) * LANE:(hd // 2 + 1) * LANE, :].astype(F32)
        qpad[hd] = jnp.where(pad_rows == hd % 2, pair, 0.0).astype(BF16)
    qit = qit_ref[0]
    zeros_idx = jnp.zeros((LANE - IDX_DIM, PQB), BF16)
    for hd in range(IDX_HEADS):
        qipad[hd] = jnp.concatenate([qit[hd * IDX_DIM:(hd + 1) * IDX_DIM, :], zeros_idx], axis=0)

    w_rows = [wt_ref[0, hd:hd + 1, :] * IDX_SCALE for hd in range(IDX_HEADS)]

    def score_chunk(c, carry):
        kc = keys(tailb_ref, c, slice(None))
        s = jnp.zeros((KC, PQB), F32)
        for hd in range(IDX_HEADS):
            s = s + jnp.maximum(_dot(kc, qipad[hd]), 0.0) * w_rows[hd]
        sc[c] = jnp.where(k_off + c * KC <= q_pos, s, NEG_INF)
        return carry

    lax.fori_loop(0, nc, score_chunk, 0)

    def count(cmp):
        def body(c, cnt):
            return cnt + fold8(jnp.where(cmp(sc[c]), 1.0, 0.0))
        return jnp.sum(lax.fori_loop(0, nc, body, jnp.zeros((8, PQB), F32)), axis=0, keepdims=True)

    thr = _kth_largest(lambda cand: count(lambda x: x >= cand), topk, (1, PQB))

    need = float(topk) - count(lambda x: x > thr)
    tri = jnp.where(lax.broadcasted_iota(I32, (KC, KC), 0) >= lax.broadcasted_iota(I32, (KC, KC), 1),
                    1.0, 0.0).astype(BF16)

    def mask_chunk(c, off):
        x = sc[c]
        eq = x == thr
        pre = off + _dot(tri, jnp.where(eq, 1.0, 0.0).astype(BF16))
        sel = ((x > thr) | (eq & (pre <= need))) & (k_off + c * KC <= q_pos)
        sc[c] = jnp.where(sel, 0.0, NEG_INF)
        return pre[KC - 1:KC, :]

    lax.fori_loop(0, nc, mask_chunk, jnp.zeros((1, PQB), F32))

    m_s[...] = jnp.full(m_s.shape, NEG_INF, F32)
    l_s[...] = jnp.zeros(l_s.shape, F32)
    acc_s[...] = jnp.zeros(acc_s.shape, F32)

    def attn_chunk(c, carry, near):
        for sb in range(KC // ATT_SUB):
            k0 = pl.multiple_of(c * KC + sb * ATT_SUB, ATT_SUB)
            mk = sc[c, sb * ATT_SUB:(sb + 1) * ATT_SUB, :]

            def logits(hd):
                pair = slice((hd // 2) * LANE, (hd // 2 + 1) * LANE)
                x = _dot(kb_ref[pl.ds(k0, ATT_SUB), pair], qpad[hd]) + mk
                if near:
                    parts = []
                    for j in range(ATT_SUB // PQB):
                        blk = (KC // PQB) * c + sb * (ATT_SUB // PQB) + j
                        delta = jnp.where(blk == i, dt[hd, PQB:, :], jnp.where(blk == i - 1, dt[hd, :PQB, :], 0.0))
                        parts.append(x[j * PQB:(j + 1) * PQB, :] + delta)
                    x = jnp.concatenate(parts, axis=0)
                return x

            m_fins, alphas = [], []
            for hd in range(N_HEADS_A):
                x = logits(hd)
                lg[hd] = x
                m_old = m_s[hd]
                m_new = jnp.maximum(m_old, jnp.max(x, axis=0, keepdims=True))
                m_fin = jnp.where(m_new == NEG_INF, 0.0, m_new)
                m_fins.append(m_fin)
                alphas.append(jnp.exp(m_old - m_fin))
                m_s[hd] = m_new
            ones_rows = jnp.ones((16, ATT_SUB), BF16)
            for hd in range(N_HEADS_A):
                p = jnp.exp(lg[hd] - m_fins[hd]).astype(BF16)
                vt_h = vt_ref[c, hd * HEAD_DIM:(hd + 1) * HEAD_DIM, sb * ATT_SUB:(sb + 1) * ATT_SUB]
                pv = _dot(jnp.concatenate([vt_h, ones_rows], axis=0), p)
                acc_s[hd] = alphas[hd] * acc_s[hd] + pv[:HEAD_DIM]
                l_s[hd] = alphas[hd] * l_s[hd] + pv[HEAD_DIM:HEAD_DIM + 1]
        return carry

    n_far = jnp.maximum(nc - 2, 0)
    lax.fori_loop(0, n_far, functools.partial(attn_chunk, near=False), 0)
    lax.fori_loop(n_far, nc, functools.partial(attn_chunk, near=True), 0)
    out_t = jnp.concatenate([acc_s[hd] / l_s[hd] for hd in range(N_HEADS_A)], axis=0)
    o_ref[...] = out_t.T


def _prompt_attn(btab, rel_bias, qt, qit, wt, tailb, kb, vt, batch, seq):
    nq = seq // PQB
    ncb = seq // KC
    per = KC // PQB
    topk = min(TOPK_MAX, seq // 4)
    qtile = lambda rows: pl.BlockSpec((1, rows, PQB), lambda b, i: (b * ncb + i // per, 0, i % per))
    return pl.pallas_call(
        functools.partial(_prompt_attn_kernel, topk=topk),
        grid=(batch, nq),
        in_specs=[pl.BlockSpec((2 * PQB, PQB), lambda b, i: (0, 0)),
                  pl.BlockSpec(memory_space=pltpu.SMEM),
                  qtile(A_WIDTH), qtile(IDX_HEADS * IDX_DIM), qtile(8),
                  _resident((seq, LANE), lambda b, i: (b, 0)),
                  _resident((seq, A_WIDTH), lambda b, i: (b, 0)),
                  _resident((ncb, A_WIDTH, KC), lambda b, i: (b, 0, 0))],
        out_specs=pl.BlockSpec((PQB, A_WIDTH), lambda b, i: (b * nq + i, 0)),
        out_shape=jax.ShapeDtypeStruct((batch * seq, A_WIDTH), F32),
        scratch_shapes=[pltpu.VMEM((ncb, KC, PQB), F32),
                        pltpu.VMEM((N_HEADS_A, 2 * PQB, PQB), F32),
                        pltpu.VMEM((N_HEADS_A, LANE, PQB), BF16),
                        pltpu.VMEM((IDX_HEADS, LANE, PQB), BF16),
                        pltpu.VMEM((N_HEADS_A, 1, PQB), F32),
                        pltpu.VMEM((N_HEADS_A, 1, PQB), F32),
                        pltpu.VMEM((N_HEADS_A, HEAD_DIM, PQB), F32),
                        pltpu.VMEM((N_HEADS_A, ATT_SUB, PQB), F32)],
        compiler_params=_cparams(2),
        name="prompt_attn",
    )(btab, rel_bias, qt, qit, wt, tailb, kb, vt)


SEL_PAGES = 16
ATT_PAGES = 16


def _sample_select_kernel(pt_ref, qi_ref, tail_ref, *rest, n_groups, t_new, topk):
    pages = rest[:SEL_PAGES]
    mpast_ref, mnew_ref, scv, scn = rest[SEL_PAGES:]
    g = pl.program_id(1)
    gw = SEL_PAGES * PAGE
    w4 = tail_ref[:, 64:68] * IDX_SCALE
    q4 = qi_ref[...].reshape(IDX_HEADS * t_new, IDX_DIM)

    def scores(d):
        s = jnp.zeros((t_new, d.shape[1]), F32)
        for hd in range(IDX_HEADS):
            s = s + jnp.maximum(d[hd * t_new:(hd + 1) * t_new, :], 0.0) * w4[:, hd:hd + 1]
        return s

    kpast_t = jnp.concatenate([p[0] for p in pages], axis=1).astype(BF16)
    scv[g] = scores(_dot(q4, kpast_t))

    @pl.when(g == n_groups - 1)
    def _():
        rown = lax.broadcasted_iota(I32, (t_new, LANE), 0)
        lanen = lax.broadcasted_iota(I32, (t_new, LANE), 1)
        knew = jnp.concatenate([tail_ref[:, 0:64], jnp.zeros((LANE - t_new, 64), F32)], axis=0).astype(BF16)
        sn = scores(_dot_t(q4, knew))
        causal_n = lanen <= rown
        scn[...] = jnp.where(causal_n, sn, NEG_INF)

        def count(cmp):
            parts = [jnp.where(cmp(scn[...]), 1.0, 0.0)]
            for gg in range(n_groups):
                hit = jnp.where(cmp(scv[gg]), 1.0, 0.0)
                parts.extend(hit[:, j * LANE:(j + 1) * LANE] for j in range(gw // LANE))
            while len(parts) > 1:
                parts = [a + b for a, b in zip(parts[::2], parts[1::2])] + parts[len(parts) & ~1:]
            return jnp.sum(parts[0], axis=-1, keepdims=True)

        thr = _kth_largest(lambda cand: count(lambda x: x >= cand), topk, (t_new, 1))
        need = float(topk) - count(lambda x: x > thr)
        tri, ones = _tri_ones()
        blocks = [(gg, j) for gg in range(n_groups) for j in range(gw // LANE)]
        tile = lambda gg, j: scv[gg, :, j * LANE:(j + 1) * LANE]
        ties = lambda x: jnp.where(x == thr, 1.0, 0.0).astype(BF16)
        off = jnp.zeros((t_new, LANE), F32)
        for gg, j in blocks:
            x = tile(gg, j)
            eqb = ties(x)
            sel = (x > thr) | ((x == thr) & (off + _dot(eqb, tri) <= need))
            mpast_ref[0, :, gg * gw + j * LANE:gg * gw + (j + 1) * LANE] = jnp.where(sel, 0.0, NEG_INF)
            off = off + jnp.sum(eqb.astype(F32), axis=-1, keepdims=True)
        x = scn[...]
        sel = ((x > thr) | ((x == thr) & (off + _dot(ties(x), tri) <= need))) & causal_n
        mnew_ref[0] = jnp.where(sel, 0.0, NEG_INF)


def _sample_select(page_table, qi_hm, tail, cache_idx_t, t_new):
    bsz, n_pages = page_table.shape
    n_groups = n_pages // SEL_PAGES
    past = n_pages * PAGE
    topk = min(TOPK_MAX, (past + t_new) // 4)

    def page_spec(j):
        return pl.BlockSpec((1, IDX_DIM, PAGE), lambda b, g, pt: (pt[b, g * SEL_PAGES + j], 0, 0))

    grid_spec = pltpu.PrefetchScalarGridSpec(
        num_scalar_prefetch=1,
        grid=(bsz, n_groups),
        in_specs=[pl.BlockSpec((IDX_HEADS, t_new, 64), lambda b, g, pt: (0, b, 0)),
                  pl.BlockSpec((t_new, 128), lambda b, g, pt: (b, 0))]
                 + [page_spec(j) for j in range(SEL_PAGES)],
        out_specs=[pl.BlockSpec((1, t_new, past), lambda b, g, pt: (b, 0, 0)),
                   pl.BlockSpec((1, t_new, LANE), lambda b, g, pt: (b, 0, 0))],
        scratch_shapes=[pltpu.VMEM((n_groups, t_new, SEL_PAGES * PAGE), F32),
                        pltpu.VMEM((t_new, LANE), F32)])
    return pl.pallas_call(
        functools.partial(_sample_select_kernel, n_groups=n_groups, t_new=t_new, topk=topk),
        grid_spec=grid_spec,
        out_shape=[jax.ShapeDtypeStruct((bsz, t_new, past), F32),
                   jax.ShapeDtypeStruct((bsz, t_new, LANE), F32)],
        compiler_params=_cparams(2),
        name="sample_select",
    )(page_table, qi_hm, tail, *([cache_idx_t] * SEL_PAGES))


def _sample_attn_kernel(pt_ref, btab_ref, rb_ref, q_ref, knew_ref, vnew_ref, mpast_ref, mnew_ref, *rest,
                        n_steps, t_new):
    kpages = rest[:ATT_PAGES]
    vpages = rest[ATT_PAGES:2 * ATT_PAGES]
    o_ref, qbd, dq, m_s, l_s, acc_s = rest[2 * ATT_PAGES:]
    g = pl.program_id(1)
    hq = N_HEADS_A * t_new
    gw = ATT_PAGES * PAGE
    head_of_row = lax.broadcasted_iota(I32, (hq, A_WIDTH), 0) // t_new
    head_of_col = lax.broadcasted_iota(I32, (hq, A_WIDTH), 1) // HEAD_DIM
    own = head_of_row == head_of_col

    @pl.when((pl.program_id(0) == 0) & (g == 0))
    def _():
        bt = btab_ref[0:t_new, :]
        for hd in range(N_HEADS_A):
            dq[hd * t_new:(hd + 1) * t_new, :] = _bias_delta(bt, rb_ref, hd)

    @pl.when(g == 0)
    def _():
        qt = jnp.concatenate([q_ref[...].astype(F32)] * N_HEADS_A, axis=0)
        qbd[...] = jnp.where(own, qt, 0.0).astype(BF16)
        m_s[...] = jnp.full((hq, 1), NEG_INF, F32)
        l_s[...] = jnp.zeros((hq, 1), F32)
        acc_s[...] = jnp.zeros((hq, A_WIDTH), F32)

    def online(x, vb_t):
        m_old = m_s[...]
        m_new = jnp.maximum(m_old, jnp.max(x, axis=-1, keepdims=True))
        m_fin = jnp.where(m_new == NEG_INF, 0.0, m_new)
        alpha = jnp.exp(m_old - m_fin)
        p = jnp.exp(x - m_fin)
        l_s[...] = alpha * l_s[...] + jnp.sum(p, axis=-1, keepdims=True)
        acc_s[...] = alpha * acc_s[...] + _dot_t(p.astype(BF16), vb_t)
        m_s[...] = m_new

    kb_t = jnp.concatenate([p[0] for p in kpages], axis=1).astype(BF16)
    vb_t = jnp.concatenate([p[0] for p in vpages], axis=1).astype(BF16)
    x = _dot(qbd[...], kb_t) + jnp.concatenate([mpast_ref[0]] * N_HEADS_A, axis=0)
    lane = lax.broadcasted_iota(I32, (hq, gw), 1)
    near = jnp.concatenate([jnp.zeros((hq, gw - LANE), F32), dq[:, :LANE]], axis=1)
    x = x + jnp.where(g == n_steps - 1, near, 0.0)

    online(x, vb_t)

    @pl.when(g == n_steps - 1)
    def _():
        pad = jnp.zeros((LANE - t_new, A_WIDTH), F32)
        kn = jnp.concatenate([knew_ref[...], pad], axis=0).astype(BF16)
        vn = jnp.concatenate([vnew_ref[...], pad], axis=0)
        xn = _dot_t(qbd[...], kn) + jnp.concatenate([mnew_ref[0]] * N_HEADS_A, axis=0) + dq[:, LANE:]
        online(xn, vn.T.astype(BF16))
        res = jnp.where(own, acc_s[...] / l_s[...], 0.0)
        out = res[0:t_new, :]
        for hd in range(1, N_HEADS_A):
            out = out + res[hd * t_new:(hd + 1) * t_new, :]
        o_ref[...] = out


def _sample_attn(page_table, btab, rel_bias, q_rows, k_new, v_new, mpast, mnew, cache_kt, cache_vt, t_new):
    bsz, n_pages = page_table.shape
    n_steps = n_pages // ATT_PAGES
    hq = N_HEADS_A * t_new

    def page_spec(j):
        return pl.BlockSpec((1, A_WIDTH, PAGE), lambda b, g, pt: (pt[b, g * ATT_PAGES + j], 0, 0))

    grid_spec = pltpu.PrefetchScalarGridSpec(
        num_scalar_prefetch=1,
        grid=(bsz, n_steps),
        in_specs=[pl.BlockSpec((QB, 2 * QB), lambda b, g, pt: (0, 0)),
                  pl.BlockSpec(memory_space=pltpu.SMEM),
                  pl.BlockSpec((t_new, A_WIDTH), lambda b, g, pt: (b, 0)),
                  pl.BlockSpec((t_new, A_WIDTH), lambda b, g, pt: (b, 0)),
                  pl.BlockSpec((t_new, A_WIDTH), lambda b, g, pt: (b, 0)),
                  pl.BlockSpec((1, t_new, ATT_PAGES * PAGE), lambda b, g, pt: (b, 0, g)),
                  pl.BlockSpec((1, t_new, LANE), lambda b, g, pt: (b, 0, 0))]
                 + [page_spec(j) for j in range(ATT_PAGES)] * 2,
        out_specs=pl.BlockSpec((t_new, A_WIDTH), lambda b, g, pt: (b, 0)),
        scratch_shapes=[pltpu.VMEM((hq, A_WIDTH), BF16),
                        pltpu.VMEM((hq, 2 * QB), F32),
                        pltpu.VMEM((hq, 1), F32),
                        pltpu.VMEM((hq, 1), F32),
                        pltpu.VMEM((hq, A_WIDTH), F32)])
    return pl.pallas_call(
        functools.partial(_sample_attn_kernel, n_steps=n_steps, t_new=t_new),
        grid_spec=grid_spec,
        out_shape=jax.ShapeDtypeStruct((bsz * t_new, A_WIDTH), F32),
        compiler_params=_cparams(2),
        name="sample_attn",
    )(page_table, btab, rel_bias, q_rows, k_new, v_new, mpast, mnew,
      *([cache_kt] * ATT_PAGES), *([cache_vt] * ATT_PAGES))


def _head_block_ones():
    r = lax.broadcasted_iota(I32, (B_WIDTH, B_WIDTH), 0) // HEAD_B
    c = lax.broadcasted_iota(I32, (B_WIDTH, B_WIDTH), 1) // HEAD_B
    return jnp.where(r == c, 1.0, 0.0).astype(BF16)


def _rwkv_prep_kernel(rw_ref, prev8_ref, prow_ref, mu_ref, w0_ref, a0_ref, kk_ref, ka_ref, rk_ref,
                      w2_ref, a2_ref, g2_ref,
                      r_ref, w_ref, k_ref, v_ref, na_ref, bb_ref, bonus_ref, g_ref, *, tt):
    t = pl.program_id(1)
    cur = rw_ref[0]
    first_prev = jnp.where(t == 0, prow_ref[0], prev8_ref[0, 7:8, :])
    rows = lax.broadcasted_iota(I32, cur.shape, 0)
    prev = jnp.where(rows == 0, first_prev, pltpu.roll(cur, 1, 0))
    mixed = cur + (prev - cur) * mu_ref[...]
    r = mixed[:, 0:512]
    k = mixed[:, 512:1024]
    v = mixed[:, 1024:1536]
    wd = mixed[:, 1536:1600]
    ad = mixed[:, 1600:1664]
    gd = mixed[:, 1664:1792]
    z = w0_ref[...] + _dot(jnp.tanh(wd).astype(BF16), w2_ref[...])
    u = -z
    softplus = jnp.maximum(u, 0.0) + jnp.log1p(jnp.exp(-jnp.abs(u)))
    decay = jnp.exp(-jnp.exp(-softplus - 0.5))
    a = jax.nn.sigmoid(a0_ref[...] + _dot(ad.astype(BF16), a2_ref[...]))
    g = _dot(jax.nn.sigmoid(gd).astype(BF16), g2_ref[...])
    bd = _head_block_ones()
    kk = k * kk_ref[...]
    nrm = jnp.sqrt(_split_dot(kk * kk, bd))
    kk = kk / jnp.maximum(nrm, 1e-12)
    k2 = k * (1.0 + (a - 1.0) * ka_ref[...])
    r_ref[0] = r
    w_ref[0] = decay
    k_ref[0] = k2
    v_ref[0] = v
    na_ref[0] = -kk
    bb_ref[0] = kk * a
    bonus_ref[0] = _split_dot(r * k2 * rk_ref[...], bd) * v
    g_ref[0] = g


def _rwkv_prep(rw3, prow, p, tt):
    bsz, t_len, _ = rw3.shape
    vec = lambda c: pl.BlockSpec((1, c), lambda b, t: (0, 0))
    full = lambda r, c: pl.BlockSpec((r, c), lambda b, t: (0, 0))
    tile = pl.BlockSpec((1, tt, B_WIDTH), lambda b, t: (b, t, 0))
    return pl.pallas_call(
        functools.partial(_rwkv_prep_kernel, tt=tt),
        grid=(bsz, t_len // tt),
        in_specs=[pl.BlockSpec((1, tt, RWKV_COLS), lambda b, t: (b, t, 0)),
                  pl.BlockSpec((1, 8, RWKV_COLS), lambda b, t: (b, jnp.maximum(t * (tt // 8) - 1, 0), 0)),
                  pl.BlockSpec((1, 1, RWKV_COLS), lambda b, t: (b, 0, 0)),
                  vec(RWKV_COLS), vec(B_WIDTH), vec(B_WIDTH), vec(B_WIDTH), vec(B_WIDTH), vec(B_WIDTH),
                  full(D_DECAY, B_WIDTH), full(D_AAA, B_WIDTH), full(D_GATE, B_WIDTH)],
        out_specs=[tile] * 8,
        out_shape=[jax.ShapeDtypeStruct((bsz, t_len, B_WIDTH), F32)] * 8,
        compiler_params=_cparams(2),
        name="rwkv_prep",
    )(rw3, rw3, prow.reshape(bsz, 1, RWKV_COLS), p["mu"], p["w0"], p["a0"], p["k_k"], p["k_a"], p["r_k"],
      p["w2"], p["a2"], p["g2"])


SCAN_BLOCK = 2


def _rwkv_scan_kernel(r_ref, w_ref, k_ref, v_ref, na_ref, bb_ref, s0_ref, y_ref, st_ref, *, bb_n, tc):
    @pl.when(pl.program_id(1) == 0)
    def _():
        st_ref[...] = s0_ref[...]

    lane = lax.broadcasted_iota(I32, (HEAD_B, LANE), 1)
    rowi = lax.broadcasted_iota(I32, (HEAD_B, LANE), 0)
    diag = (lane & (HEAD_B - 1)) == rowi
    step_lane = lane & (HEAD_B - 1)
    bj = lax.broadcasted_iota(I32, (2 * LANE, LANE), 0)
    bl = lax.broadcasted_iota(I32, (2 * LANE, LANE), 1)
    bd2 = jnp.where((bj & (LANE - 1)) // HEAD_B == bl // HEAD_B, 1.0, 0.0).astype(BF16)
    bd1 = bd2[0:LANE]

    def hi_lo(x):
        hi = x.astype(BF16)
        return hi, (x - hi.astype(F32)).astype(BF16)

    chains = [(bi, pr) for bi in range(bb_n) for pr in range(N_HEADS_B // 2)]

    def tile(t8, carry):
        t0 = pl.multiple_of(t8 * 8, 8)
        ins = {c: tuple(ref[c[0], pl.ds(t0, 8), c[1] * LANE:(c[1] + 1) * LANE]
                        for ref in (r_ref, w_ref, k_ref, v_ref, na_ref, bb_ref)) for c in chains}
        state = {c: st_ref[c[0], c[1]] for c in chains}
        ycols = {c: jnp.zeros((HEAD_B, LANE), F32) for c in chains}
        nch = len(chains)
        piece = lambda x, j: x[j * HEAD_B:(j + 1) * HEAD_B]

        def shift_up(x, d):
            return pltpu.roll(x, 8 - d, 0)

        prods = []
        for d in range(1, SCAN_BLOCK):
            for n in (5, 2):
                for c in chains:
                    x = ins[c][n] * shift_up(ins[c][4], d)
                    for u in range(1, d):
                        x = x * shift_up(ins[c][1], u)
                    prods.append(x)
        if prods:
            dots = _dot(jnp.concatenate(hi_lo(jnp.concatenate(prods, axis=0)), axis=1), bd2)

        def coef(d, which, j, l):
            row = (((d - 1) * 2 + which) * nch + j) * 8 + l
            return dots[row:row + 1, :]

        for i0 in range(0, 8, SCAN_BLOCK):
            steps = range(i0, i0 + SCAN_BLOCK)
            row = lambda c, n, i: ins[c][n][i:i + 1, :]
            p_all = []
            for i in steps:
                for c in chains:
                    at = row(c, 4, i)
                    for u in range(i0, i):
                        at = at * row(c, 1, u)
                    p_all.append(jnp.concatenate(hi_lo(state[c] * at), axis=1))
            sab = _dot(jnp.concatenate(p_all, axis=0), bd2)
            z_all = [jnp.where(diag, row(c, 3, i), 0.0).astype(BF16) for i in steps for c in chains]
            vcs = _dot(jnp.concatenate(z_all, axis=0), bd1)
            y_all = []
            for j, c in enumerate(chains):
                s = state[c]
                sas = []
                for m, i in enumerate(steps):
                    sa = piece(sab, m * nch + j)
                    for ml in range(m):
                        sa = sa + sas[ml] * coef(m - ml, 0, j, i0 + ml) \
                            + piece(vcs, ml * nch + j) * coef(m - ml, 1, j, i0 + ml)
                    sas.append(sa)
                    s = s * row(c, 1, i) + sa * row(c, 5, i) + piece(vcs, m * nch + j) * row(c, 2, i)
                    y_all.append((s * row(c, 0, i)).astype(BF16))
                state[c] = s
            yb_all = _dot(jnp.concatenate(y_all, axis=0), bd1)
            for j, c in enumerate(chains):
                for m, i in enumerate(steps):
                    ycols[c] = jnp.where(step_lane == i, piece(yb_all, j * SCAN_BLOCK + m), ycols[c])
        for c in chains:
            st_ref[c[0], c[1]] = state[c]
            yt = ycols[c].T
            y_ref[c[0], pl.ds(t0, 8), c[1] * LANE:c[1] * LANE + HEAD_B] = yt[0:8]
            y_ref[c[0], pl.ds(t0, 8), c[1] * LANE + HEAD_B:(c[1] + 1) * LANE] = yt[HEAD_B:HEAD_B + 8]
        return carry

    lax.fori_loop(0, tc // 8, tile, 0)


def _pair_pack(state):
    b = state.shape[0]
    s = state.reshape(b, N_HEADS_B // 2, 2, HEAD_B, HEAD_B)
    return jnp.transpose(s, (0, 1, 3, 2, 4)).reshape(b, N_HEADS_B // 2, HEAD_B, 2 * HEAD_B)


def _pair_unpack(state):
    b = state.shape[0]
    s = state.reshape(b, N_HEADS_B // 2, HEAD_B, 2, HEAD_B)
    return jnp.transpose(s, (0, 1, 3, 2, 4)).reshape(b, N_HEADS_B, HEAD_B, HEAD_B)


def _rwkv_scan(r, w, k, v, na, bb, state0, bb_n, tc):
    bsz, t_len, _ = r.shape
    tile = pl.BlockSpec((bb_n, tc, B_WIDTH), lambda b, t: (b, t, 0))
    st = pl.BlockSpec((bb_n, N_HEADS_B // 2, HEAD_B, 2 * HEAD_B), lambda b, t: (b, 0, 0, 0))
    y, state = pl.pallas_call(
        functools.partial(_rwkv_scan_kernel, bb_n=bb_n, tc=tc),
        grid=(bsz // bb_n, t_len // tc),
        in_specs=[tile] * 6 + [st],
        out_specs=[tile, st],
        out_shape=[jax.ShapeDtypeStruct((bsz, t_len, B_WIDTH), F32),
                   jax.ShapeDtypeStruct((bsz, N_HEADS_B // 2, HEAD_B, 2 * HEAD_B), F32)],
        compiler_params=_cparams(2),
        name="rwkv_scan",
    )(r, w, k, v, na, bb, _pair_pack(state0))
    return y, _pair_unpack(state)


def _tail_kernel(x_ref, att_ref, y_ref, bonus_ref, g_ref, gate_ref, lnw_ref, lnb_ref, nm_ref, nf_ref,
                 wa_ref, wb_ref, wo_ref, w1_ref, w2_ref, o_ref):
    bd = _head_block_ones()
    y = y_ref[...]
    mean = _split_dot(y, bd) * (1.0 / HEAD_B)
    yc = y - mean
    var = _split_dot(yc * yc, bd) * (1.0 / HEAD_B)
    yn = yc * lax.rsqrt(var + GN_EPS) * lnw_ref[...] + lnb_ref[...]
    rwkv = (yn + bonus_ref[...]) * g_ref[...]
    ga = gate_ref[:, 0:D_MODEL]
    gb = gate_ref[:, D_MODEL:2 * D_MODEL]
    merged = (jax.nn.sigmoid(ga) * _dot(att_ref[...].astype(BF16), wa_ref[...])
              + jax.nn.sigmoid(gb) * _dot(rwkv.astype(BF16), wb_ref[...]))
    x1 = x_ref[...] + _dot(merged.astype(BF16), wo_ref[...])
    hm = _rms(x1, nm_ref[...]).astype(BF16)
    up = jnp.maximum(_dot(hm, w1_ref[...]), 0.0)
    x2 = x1 + _dot((up * up).astype(BF16), w2_ref[...])
    o_ref[...] = _rms(x2, nf_ref[...])


def _tail(x2d, att, y, bonus, g, gates, p, tm):
    n = x2d.shape[0]
    row = lambda i: (i, 0)
    vec = lambda c: pl.BlockSpec((1, c), lambda i: (0, 0))
    res = lambda r, c: _resident((r, c), lambda i: (0, 0))
    return pl.pallas_call(
        _tail_kernel,
        grid=(n // tm,),
        in_specs=[pl.BlockSpec((tm, D_MODEL), row),
                  pl.BlockSpec((tm, A_WIDTH), row), pl.BlockSpec((tm, B_WIDTH), row),
                  pl.BlockSpec((tm, B_WIDTH), row), pl.BlockSpec((tm, B_WIDTH), row),
                  pl.BlockSpec((tm, 2 * D_MODEL), row),
                  vec(B_WIDTH), vec(B_WIDTH), vec(D_MODEL), vec(D_MODEL),
                  res(A_WIDTH, D_MODEL), res(B_WIDTH, D_MODEL), res(D_MODEL, D_MODEL),
                  res(D_MODEL, D_FF), res(D_FF, D_MODEL)],
        out_specs=pl.BlockSpec((tm, D_MODEL), row),
        out_shape=jax.ShapeDtypeStruct((n, D_MODEL), F32),
        compiler_params=_cparams(1),
        name="tail",
    )(x2d, att, y, bonus, g, gates, p["ln_w"], p["ln_b"], p["norm_mlp"], p["norm_final"],
      p["wa"], p["wb"], p["wo"], p["w1"], p["w2m"])


def _prepare_params(norm_mix, w_in, rwkv_mu, rwkv_w0, rwkv_w2, rwkv_a0, rwkv_a2, rwkv_g2, rwkv_k_k, rwkv_k_a,
                    rwkv_r_k, rwkv_ln_w, rwkv_ln_b, w_branch_a, w_branch_b, w_out, norm_mlp, w_mlp_in,
                    w_mlp_out, norm_final):
    wb16 = w_in.astype(BF16)
    o_idx = 3 * A_WIDTH
    o_kidx = o_idx + IDX_HEADS * IDX_DIM
    o_widx = o_kidx + IDX_DIM
    pad = jnp.zeros((D_MODEL, 128 - IDX_DIM - IDX_HEADS), BF16)
    w_attn = jnp.concatenate([wb16[:, :o_kidx], wb16[:, o_kidx:o_widx + IDX_HEADS], pad], axis=1)
    o = RWKV_OFF
    def regroup(a):
        return jnp.concatenate([a[..., 0:512], a[..., 576:1600], a[..., 512:576], a[..., 1600:1792]], axis=-1)
    w_rw = regroup(wb16[:, o:o + RWKV_COLS])
    w_rest = jnp.concatenate([w_rw, wb16[:, o + RWKV_COLS:]], axis=1)
    r1 = lambda a: a.reshape(1, -1).astype(F32)
    return dict(
        norm_mix=r1(norm_mix), w_attn=w_attn, w_rest=w_rest, w_rw=w_rw,
        mu=r1(regroup(rwkv_mu)), w0=r1(rwkv_w0), a0=r1(rwkv_a0), k_k=r1(rwkv_k_k), k_a=r1(rwkv_k_a),
        r_k=r1(rwkv_r_k), w2=rwkv_w2.astype(BF16), a2=rwkv_a2.astype(BF16), g2=rwkv_g2.astype(BF16),
        ln_w=r1(rwkv_ln_w), ln_b=r1(rwkv_ln_b), norm_mlp=r1(norm_mlp), norm_final=r1(norm_final),
        wa=w_branch_a.astype(BF16), wb=w_branch_b.astype(BF16), wo=w_out.astype(BF16),
        w1=w_mlp_in.astype(BF16), w2m=w_mlp_out.astype(BF16))


def _layer(x, shift_rows, wkv0, attend, p, tm, tm_tail, scan_tc):
    bsz, t_len, _ = x.shape
    n = bsz * t_len
    x2d = x.reshape(n, D_MODEL)
    hl_rows = 8 if t_len >= tm else tm
    names = ("q_hm", "k", "v", "qt", "kb", "vt", "qi_hm", "qit", "tail", "tailb", "wt", "hlast")
    long_seq = t_len >= tm
    pr = dict(zip(names, _proj_attn(x2d, p["norm_mix"], p["w_attn"], tm, hl_rows, t_len // tm if long_seq else 0)))
    tail, hlast = pr["tail"], pr["hlast"]
    if long_seq:
        heads = lambda a: jnp.transpose(a.reshape(bsz, N_HEADS_A, HEAD_DIM, t_len), (0, 3, 1, 2))
    else:
        heads = lambda a: a.reshape(bsz, t_len, N_HEADS_A, HEAD_DIM)
    rw, gates = _proj_rest(x2d, p["norm_mix"], p["w_rest"], tm)
    att = attend(pr)
    prow = _shift_proj(shift_rows, p["w_rw"])
    r, w, k2, v2, na, bb, bonus, g = _rwkv_prep(rw.reshape(bsz, t_len, RWKV_COLS), prow, p, min(256, t_len))
    y, wkv = _rwkv_scan(r, w, k2, v2, na, bb, wkv0, 2, scan_tc)
    flat = lambda a: a.reshape(n, B_WIDTH)
    out = _tail(x2d, att, flat(y), flat(bonus), flat(g), gates, p, tm_tail)
    if long_seq:
        shift = hlast.reshape(bsz, t_len // tm, 8, D_MODEL)[:, -1, 7]
    else:
        shift = hlast.reshape(bsz, t_len, D_MODEL)[:, -1]
    return (out.reshape(bsz, t_len, D_MODEL), heads(pr["k"]), heads(pr["v"]),
            tail[:, :IDX_DIM].reshape(bsz, t_len, IDX_DIM), wkv, shift)


def kernel(x_prompt, x_sample, cache_k, cache_v, cache_idx_k, state_wkv, state_shift, page_table, rel_bias, norm_mix, w_in, rwkv_mu, rwkv_w0, rwkv_w2, rwkv_a0, rwkv_a2, rwkv_g2, rwkv_k_k, rwkv_k_a, rwkv_r_k, rwkv_ln_w, rwkv_ln_b, w_branch_a, w_branch_b, w_out, norm_mlp, w_mlp_in, w_mlp_out, norm_final):
    p = _prepare_params(norm_mix, w_in, rwkv_mu, rwkv_w0, rwkv_w2, rwkv_a0, rwkv_a2, rwkv_g2, rwkv_k_k,
                        rwkv_k_a, rwkv_r_k, rwkv_ln_w, rwkv_ln_b, w_branch_a, w_branch_b, w_out, norm_mlp,
                        w_mlp_in, w_mlp_out, norm_final)
    btab = jnp.asarray(_window_buckets())
    btab_t = jnp.asarray(_window_buckets_t())
    b_p, s_p, _ = x_prompt.shape
    b_s, t_s, _ = x_sample.shape
    n_pool = cache_k.shape[0]

    def prompt_attend(pr):
        return _prompt_attn(btab_t, rel_bias, pr["qt"], pr["qit"], pr["wt"], pr["tailb"], pr["kb"], pr["vt"],
                            b_p, s_p)

    def sample_attend(pr):
        key_minor = lambda c: jnp.transpose(c.reshape(n_pool, PAGE, -1), (0, 2, 1))
        mpast, mnew = _sample_select(page_table, pr["qi_hm"], pr["tail"], key_minor(cache_idx_k), t_s)
        q_rows = jnp.transpose(pr["q_hm"], (1, 0, 2)).reshape(b_s * t_s, A_WIDTH)
        return _sample_attn(page_table, btab, rel_bias, q_rows, pr["k"], pr["v"], mpast, mnew,
                            key_minor(cache_k), key_minor(cache_v), t_s)

    zero_shift = jnp.zeros((b_p, D_MODEL), F32)
    zero_wkv = jnp.zeros((b_p, N_HEADS_B, HEAD_B, HEAD_B), F32)
    y_p, k_p, v_p, ik_p, wkv_p, sh_p = _layer(x_prompt, zero_shift, zero_wkv, prompt_attend, p, KC, 256, 256)
    y_s, k_s, v_s, ik_s, wkv_s, sh_s = _layer(x_sample, state_shift, state_wkv, sample_attend, p,
                                              b_s * t_s, b_s * t_s, t_s)
    return (y_p, y_s, k_p, v_p, ik_p, wkv_p, sh_p, k_s, v_s, ik_s, wkv_s, sh_s)
```

```python
import functools
import math

import numpy as np
import jax
import jax.numpy as jnp
from jax import lax
from jax.experimental import pallas as pl
from jax.experimental.pallas import tpu as pltpu

F32 = jnp.float32
BF16 = jnp.bfloat16
I32 = jnp.int32

D_MODEL = 1024
PAGE = 128
HEAD_DIM = 64
A_WIDTH = 512
N_HEADS_A = 8
IDX_HEADS = 4
IDX_DIM = 64
IDX_SCALE = (IDX_HEADS * IDX_DIM) ** -0.5
ATT_SCALE = HEAD_DIM ** -0.5
TOPK_MAX = 256
REL_BUCKETS = 32
REL_MAX_DIST = 128
HEAD_B = 64
B_WIDTH = 512
N_HEADS_B = 8
D_DECAY = 64
D_AAA = 64
D_GATE = 128
GN_EPS = 64e-5
D_FF = 4096
RMS_EPS = 1e-6
RWKV_COLS = 3 * B_WIDTH + D_DECAY + D_AAA + D_GATE
RWKV_OFF = 3 * A_WIDTH + IDX_HEADS * IDX_DIM + IDX_DIM + IDX_HEADS

LANE = 128
QB = 128
PQB = 256
KC = 512
ATT_SUB = KC
NEG_INF = float("-inf")
INT_MIN = -(2 ** 31)
VMEM_LIMIT = 56 * 1024 * 1024


def _cparams(n_axes):
    return pltpu.CompilerParams(dimension_semantics=("arbitrary",) * n_axes,
                                vmem_limit_bytes=VMEM_LIMIT)


def _resident(shape, index_map):
    return pl.BlockSpec(shape, index_map, pipeline_mode=pl.Buffered(1))


def _bucket_table(dist):
    dist = np.asarray(dist, np.int64)
    max_exact = REL_BUCKETS // 2
    d_f = np.maximum(dist, max_exact).astype(np.float32)
    large = max_exact + (np.log(d_f / np.float32(max_exact)) / np.float32(math.log(REL_MAX_DIST / max_exact))
                         * np.float32(REL_BUCKETS - max_exact)).astype(np.int32)
    large = np.minimum(large, REL_BUCKETS - 1)
    return np.where(dist < max_exact, dist, large).astype(np.int32)


def _window_buckets():
    t = np.arange(QB)[:, None]
    c = np.arange(2 * QB)[None, :]
    return _bucket_table(np.maximum(QB + t - c, 0))


def _window_buckets_t():
    k = np.arange(2 * PQB)[:, None]
    q = np.arange(PQB)[None, :]
    return _bucket_table(np.maximum(PQB + q - k, 0))


def _rms(x, g):
    return x * lax.rsqrt(jnp.mean(x * x, axis=-1, keepdims=True) + RMS_EPS) * g


def _dot(a, b):
    return jnp.dot(a, b, preferred_element_type=F32)


def _dot_t(a, b):
    return lax.dot_general(a, b, (((1,), (1,)), ((), ())), preferred_element_type=F32)


def _split_dot(x, m_bf16):
    hi = x.astype(BF16)
    lo = (x - hi.astype(F32)).astype(BF16)
    return _dot(hi, m_bf16) + _dot(lo, m_bf16)


def _kth_largest(count_ge, k, shape):
    def bit_step(it, t_u):
        cand_u = t_u | jnp.left_shift(jnp.int32(1), 31 - it)
        return jnp.where(count_ge(_ordered_f32(cand_u)) >= float(k), cand_u, t_u)
    return _ordered_f32(lax.fori_loop(0, 32, bit_step, jnp.zeros(shape, I32)))


def _ordered_f32(u):
    s = u ^ INT_MIN
    bits = jnp.where(s < 0, INT_MIN | (-s), s)
    return pltpu.bitcast(bits, F32)


def _proj_attn_kernel(x_ref, g_ref, w_ref, q_ref, k_ref, v_ref, qt_ref, kb_ref, vt_ref, qi_ref, qit_ref,
                      tail_ref, tailb_ref, wt_ref, hl_ref, *, tm, hl_rows, kv_feature_major):
    h = _rms(x_ref[...], g_ref[...])
    hl_ref[...] = h[tm - hl_rows:, :]
    hb = h.astype(BF16)
    q = _dot(hb, w_ref[:, 0:512]) * ATT_SCALE
    for hd in range(N_HEADS_A):
        q_ref[hd] = q[:, hd * 64:(hd + 1) * 64].astype(BF16)
    qt_ref[0] = q.T.astype(BF16)
    k = _dot(hb, w_ref[:, 512:1024])
    kb_ref[...] = k.astype(BF16)
    v = _dot(hb, w_ref[:, 1024:1536])
    v_t = v.T
    vt_ref[0] = v_t.astype(BF16)
    if kv_feature_major:
        k_ref[0] = k.T
        v_ref[0] = v_t
    else:
        k_ref[...] = k
        v_ref[...] = v
    qi = _dot(hb, w_ref[:, 1536:1792])
    for hd in range(IDX_HEADS):
        qi_ref[hd] = qi[:, hd * 64:(hd + 1) * 64].astype(BF16)
    qit_ref[0] = qi.T.astype(BF16)
    tail = _dot(hb, w_ref[:, 1792:1920])
    tail_ref[...] = tail
    tailb_ref[...] = tail.astype(BF16)
    wt_ref[0] = tail.T[IDX_DIM:IDX_DIM + 8, :]


def _proj_attn(x2d, g, w_attn, tm, hl_rows, seq_tiles):
    n = x2d.shape[0]
    nt = n // tm
    row = lambda i: (i, 0)
    if seq_tiles:
        kv_spec = pl.BlockSpec((1, 512, tm), lambda i: (i // seq_tiles, 0, i % seq_tiles))
        kv_shape = jax.ShapeDtypeStruct((nt // seq_tiles, 512, seq_tiles * tm), F32)
    else:
        kv_spec = pl.BlockSpec((tm, 512), row)
        kv_shape = jax.ShapeDtypeStruct((n, 512), F32)
    return pl.pallas_call(
        functools.partial(_proj_attn_kernel, tm=tm, hl_rows=hl_rows, kv_feature_major=bool(seq_tiles)),
        grid=(nt,),
        in_specs=[pl.BlockSpec((tm, D_MODEL), row),
                  pl.BlockSpec((1, D_MODEL), lambda i: (0, 0)),
                  _resident((D_MODEL, 1920), lambda i: (0, 0))],
        out_specs=[pl.BlockSpec((N_HEADS_A, tm, 64), lambda i: (0, i, 0)),
                   kv_spec,
                   kv_spec,
                   pl.BlockSpec((1, 512, tm), lambda i: (i, 0, 0)),
                   pl.BlockSpec((tm, 512), row),
                   pl.BlockSpec((1, 512, tm), lambda i: (i, 0, 0)),
                   pl.BlockSpec((IDX_HEADS, tm, 64), lambda i: (0, i, 0)),
                   pl.BlockSpec((1, 256, tm), lambda i: (i, 0, 0)),
                   pl.BlockSpec((tm, 128), row),
                   pl.BlockSpec((tm, 128), row),
                   pl.BlockSpec((1, 8, tm), lambda i: (i, 0, 0)),
                   pl.BlockSpec((hl_rows, D_MODEL), row)],
        out_shape=[jax.ShapeDtypeStruct((N_HEADS_A, n, 64), BF16),
                   kv_shape,
                   kv_shape,
                   jax.ShapeDtypeStruct((nt, 512, tm), BF16),
                   jax.ShapeDtypeStruct((n, 512), BF16),
                   jax.ShapeDtypeStruct((nt, 512, tm), BF16),
                   jax.ShapeDtypeStruct((IDX_HEADS, n, 64), BF16),
                   jax.ShapeDtypeStruct((nt, 256, tm), BF16),
                   jax.ShapeDtypeStruct((n, 128), F32),
                   jax.ShapeDtypeStruct((n, 128), BF16),
                   jax.ShapeDtypeStruct((nt, 8, tm), F32),
                   jax.ShapeDtypeStruct((nt * hl_rows, D_MODEL), F32)],
        compiler_params=_cparams(1),
        name="proj_attn",
    )(x2d, g, w_attn)


def _proj_rest_kernel(x_ref, g_ref, w_ref, rw_ref, gate_ref):
    hb = _rms(x_ref[...], g_ref[...]).astype(BF16)
    rw_ref[...] = _dot(hb, w_ref[:, 0:RWKV_COLS])
    gate_ref[...] = _dot(hb, w_ref[:, RWKV_COLS:RWKV_COLS + 2 * D_MODEL])


def _proj_rest(x2d, g, w_rest, tm):
    n = x2d.shape[0]
    row = lambda i: (i, 0)
    return pl.pallas_call(
        _proj_rest_kernel,
        grid=(n // tm,),
        in_specs=[pl.BlockSpec((tm, D_MODEL), row),
                  pl.BlockSpec((1, D_MODEL), lambda i: (0, 0)),
                  _resident((D_MODEL, RWKV_COLS + 2 * D_MODEL), lambda i: (0, 0))],
        out_specs=[pl.BlockSpec((tm, RWKV_COLS), row),
                   pl.BlockSpec((tm, 2 * D_MODEL), row)],
        out_shape=[jax.ShapeDtypeStruct((n, RWKV_COLS), F32),
                   jax.ShapeDtypeStruct((n, 2 * D_MODEL), F32)],
        compiler_params=_cparams(1),
        name="proj_rest",
    )(x2d, g, w_rest)


def _shift_proj_kernel(s_ref, w_ref, o_ref):
    o_ref[...] = _dot(s_ref[...].astype(BF16), w_ref[...])


def _shift_proj(shift_rows, w_rw):
    b = shift_rows.shape[0]
    return pl.pallas_call(
        _shift_proj_kernel,
        out_shape=jax.ShapeDtypeStruct((b, RWKV_COLS), F32),
        compiler_params=pltpu.CompilerParams(vmem_limit_bytes=VMEM_LIMIT),
        name="shift_proj",
    )(shift_rows, w_rw)


def _bias_delta(btab, rb_ref, hd):
    out = jnp.zeros(btab.shape, F32)
    far = rb_ref[REL_BUCKETS - 1, hd]
    for j in range(REL_BUCKETS - 1):
        out = jnp.where(btab == j, rb_ref[j, hd] - far, out)
    return out


def _tri_ones():
    r = lax.broadcasted_iota(I32, (LANE, LANE), 0)
    c = lax.broadcasted_iota(I32, (LANE, LANE), 1)
    return jnp.where(r <= c, 1.0, 0.0).astype(BF16), jnp.ones((LANE, LANE), BF16)


def _prompt_attn_kernel(btab_ref, rb_ref, qt_ref, qit_ref, wt_ref, tailb_ref, kb_ref, vt_ref, o_ref,
                        sc, dt, qpad, qipad, m_s, l_s, acc_s, lg, *, topk):
    b = pl.program_id(0)
    i = pl.program_id(1)
    nc = (i * PQB + PQB - 1) // KC + 1

    @pl.when((b == 0) & (i == 0))
    def _():
        bt = btab_ref[...]
        for hd in range(N_HEADS_A):
            dt[hd] = _bias_delta(bt, rb_ref, hd)

    q_pos = lax.broadcasted_iota(I32, (KC, PQB), 1) + i * PQB
    k_off = lax.broadcasted_iota(I32, (KC, PQB), 0)

    def keys(ref, c, cols):
        return ref[pl.ds(pl.multiple_of(c * KC, KC), KC), cols]

    def fold8(x):
        rows = x.shape[0]
        while rows > 8:
            rows //= 2
            x = x[:rows, :] + x[rows:, :]
        return x

    pad_rows = lax.broadcasted_iota(I32, (LANE, PQB), 0) // HEAD_DIM
    qt = qt_ref[0]
    for hd in range(N_HEADS_A):
        pair = qt[(hd // 2) * LANE:(hd // 2 + 1) * LANE, :].astype(F32)
        qpad[hd] = jnp.where(pad_rows == hd % 2, pair, 0.0).astype(BF16)
    qit = qit_ref[0]
    zeros_idx = jnp.zeros((LANE - IDX_DIM, PQB), BF16)
    for hd in range(IDX_HEADS):
        qipad[hd] = jnp.concatenate([qit[hd * IDX_DIM:(hd + 1) * IDX_DIM, :], zeros_idx], axis=0)

    w_rows = [wt_ref[0, hd:hd + 1, :] * IDX_SCALE for hd in range(IDX_HEADS)]

    def score_chunk(c, carry):
        kc = keys(tailb_ref, c, slice(None))
        s = jnp.zeros((KC, PQB), F32)
        for hd in range(IDX_HEADS):
            s = s + jnp.maximum(_dot(kc, qipad[hd]), 0.0) * w_rows[hd]
        sc[c] = jnp.where(k_off + c * KC <= q_pos, s, NEG_INF)
        return carry

    lax.fori_loop(0, nc, score_chunk, 0)

    @pl.when(nc % 2 == 1)
    def _():
        sc[nc] = jnp.full((KC, PQB), NEG_INF, F32)

    def count(cmp):
        n_acc = 4

        def body(cp, accs):
            accs = list(accs)
            for half in range(2):
                for r in range(KC // 8):
                    x = sc[2 * cp + half, r * 8:(r + 1) * 8, :]
                    accs[r % n_acc] = accs[r % n_acc] + jnp.where(cmp(x), 1.0, 0.0)
            return tuple(accs)
        accs = lax.fori_loop(0, (nc + 1) // 2, body, tuple(jnp.zeros((8, PQB), F32) for _ in range(n_acc)))
        return jnp.sum(sum(accs[1:], accs[0]), axis=0, keepdims=True)

    thr = _kth_largest(lambda cand: count(lambda x: x >= cand), topk, (1, PQB))

    need = float(topk) - count(lambda x: x > thr)
    tri = jnp.where(lax.broadcasted_iota(I32, (KC, KC), 0) >= lax.broadcasted_iota(I32, (KC, KC), 1),
                    1.0, 0.0).astype(BF16)

    def mask_chunk(c, off):
        x = sc[c]
        eq = x == thr
        pre = off + _dot(tri, jnp.where(eq, 1.0, 0.0).astype(BF16))
        sel = ((x > thr) | (eq & (pre <= need))) & (k_off + c * KC <= q_pos)
        sc[c] = jnp.where(sel, 0.0, NEG_INF)
        return pre[KC - 1:KC, :]

    lax.fori_loop(0, nc, mask_chunk, jnp.zeros((1, PQB), F32))

    m_s[...] = jnp.full(m_s.shape, NEG_INF, F32)
    l_s[...] = jnp.zeros(l_s.shape, F32)
    acc_s[...] = jnp.zeros(acc_s.shape, F32)

    def attn_chunk(c, carry, near):
        for sb in range(KC // ATT_SUB):
            k0 = pl.multiple_of(c * KC + sb * ATT_SUB, ATT_SUB)
            mk = sc[c, sb * ATT_SUB:(sb + 1) * ATT_SUB, :]

            def logits(hd):
                pair = slice((hd // 2) * LANE, (hd // 2 + 1) * LANE)
                x = _dot(kb_ref[pl.ds(k0, ATT_SUB), pair], qpad[hd]) + mk
                if near:
                    parts = []
                    for j in range(ATT_SUB // PQB):
                        blk = (KC // PQB) * c + sb * (ATT_SUB // PQB) + j
                        delta = jnp.where(blk == i, dt[hd, PQB:, :], jnp.where(blk == i - 1, dt[hd, :PQB, :], 0.0))
                        parts.append(x[j * PQB:(j + 1) * PQB, :] + delta)
                    x = jnp.concatenate(parts, axis=0)
                return x

            m_fins, alphas = [], []
            for hd in range(N_HEADS_A):
                x = logits(hd)
                lg[hd] = x
                m_old = m_s[hd]
                m_new = jnp.maximum(m_old, jnp.max(x, axis=0, keepdims=True))
                m_fin = jnp.where(m_new == NEG_INF, 0.0, m_new)
                m_fins.append(m_fin)
                alphas.append(jnp.exp(m_old - m_fin))
                m_s[hd] = m_new
            ones_rows = jnp.ones((16, ATT_SUB), BF16)
            for hd in range(N_HEADS_A):
                p = jnp.exp(lg[hd] - m_fins[hd]).astype(BF16)
                vt_h = vt_ref[c, hd * HEAD_DIM:(hd + 1) * HEAD_DIM, sb * ATT_SUB:(sb + 1) * ATT_SUB]
                pv = _dot(jnp.concatenate([vt_h, ones_rows], axis=0), p)
                acc_s[hd] = alphas[hd] * acc_s[hd] + pv[:HEAD_DIM]
                l_s[hd] = alphas[hd] * l_s[hd] + pv[HEAD_DIM:HEAD_DIM + 1]
        return carry

    n_far = jnp.maximum(nc - 2, 0)
    lax.fori_loop(0, n_far, functools.partial(attn_chunk, near=False), 0)
    lax.fori_loop(n_far, nc, functools.partial(attn_chunk, near=True), 0)
    out_t = jnp.concatenate([acc_s[hd] / l_s[hd] for hd in range(N_HEADS_A)], axis=0)
    o_ref[...] = out_t.T


def _prompt_attn(btab, rel_bias, qt, qit, wt, tailb, kb, vt, batch, seq):
    nq = seq // PQB
    ncb = seq // KC
    per = KC // PQB
    topk = min(TOPK_MAX, seq // 4)
    qtile = lambda rows: pl.BlockSpec((1, rows, PQB), lambda b, i: (b * ncb + i // per, 0, i % per))
    return pl.pallas_call(
        functools.partial(_prompt_attn_kernel, topk=topk),
        grid=(batch, nq),
        in_specs=[pl.BlockSpec((2 * PQB, PQB), lambda b, i: (0, 0)),
                  pl.BlockSpec(memory_space=pltpu.SMEM),
                  qtile(A_WIDTH), qtile(IDX_HEADS * IDX_DIM), qtile(8),
                  _resident((seq, LANE), lambda b, i: (b, 0)),
                  _resident((seq, A_WIDTH), lambda b, i: (b, 0)),
                  _resident((ncb, A_WIDTH, KC), lambda b, i: (b, 0, 0))],
        out_specs=pl.BlockSpec((PQB, A_WIDTH), lambda b, i: (b * nq + i, 0)),
        out_shape=jax.ShapeDtypeStruct((batch * seq, A_WIDTH), F32),
        scratch_shapes=[pltpu.VMEM((ncb, KC, PQB), F32),
                        pltpu.VMEM((N_HEADS_A, 2 * PQB, PQB), F32),
                        pltpu.VMEM((N_HEADS_A, LANE, PQB), BF16),
                        pltpu.VMEM((IDX_HEADS, LANE, PQB), BF16),
                        pltpu.VMEM((N_HEADS_A, 1, PQB), F32),
                        pltpu.VMEM((N_HEADS_A, 1, PQB), F32),
                        pltpu.VMEM((N_HEADS_A, HEAD_DIM, PQB), F32),
                        pltpu.VMEM((N_HEADS_A, ATT_SUB, PQB), F32)],
        compiler_params=_cparams(2),
        name="prompt_attn",
    )(btab, rel_bias, qt, qit, wt, tailb, kb, vt)


SEL_PAGES = 32
ATT_PAGES = 16


def _sample_select_kernel(pt_ref, qi_ref, tail_ref, *rest, n_groups, t_new, topk):
    pages = rest[:SEL_PAGES]
    mpast_ref, mnew_ref, scv, scn = rest[SEL_PAGES:]
    g = pl.program_id(1)
    gw = SEL_PAGES * PAGE
    w4 = tail_ref[:, 64:68] * IDX_SCALE
    q4 = qi_ref[...].reshape(IDX_HEADS * t_new, IDX_DIM)

    def scores(d):
        s = jnp.zeros((t_new, d.shape[1]), F32)
        for hd in range(IDX_HEADS):
            s = s + jnp.maximum(d[hd * t_new:(hd + 1) * t_new, :], 0.0) * w4[:, hd:hd + 1]
        return s

    kpast_t = jnp.concatenate([p[0] for p in pages], axis=1).astype(BF16)
    scv[g] = scores(_dot(q4, kpast_t))

    @pl.when(g == n_groups - 1)
    def _():
        rown = lax.broadcasted_iota(I32, (t_new, LANE), 0)
        lanen = lax.broadcasted_iota(I32, (t_new, LANE), 1)
        knew = jnp.concatenate([tail_ref[:, 0:64], jnp.zeros((LANE - t_new, 64), F32)], axis=0).astype(BF16)
        sn = scores(_dot_t(q4, knew))
        causal_n = lanen <= rown
        scn[...] = jnp.where(causal_n, sn, NEG_INF)

        def count(cmp):
            parts = [jnp.where(cmp(scn[...]), 1.0, 0.0)]
            for gg in range(n_groups):
                hit = jnp.where(cmp(scv[gg]), 1.0, 0.0)
                parts.extend(hit[:, j * LANE:(j + 1) * LANE] for j in range(gw // LANE))
            while len(parts) > 1:
                parts = [a + b for a, b in zip(parts[::2], parts[1::2])] + parts[len(parts) & ~1:]
            return jnp.sum(parts[0], axis=-1, keepdims=True)

        thr = _kth_largest(lambda cand: count(lambda x: x >= cand), topk, (t_new, 1))
        need = float(topk) - count(lambda x: x > thr)
        tri, ones = _tri_ones()
        blocks = [(gg, j) for gg in range(n_groups) for j in range(gw // LANE)]
        tile = lambda gg, j: scv[gg, :, j * LANE:(j + 1) * LANE]
        ties = lambda x: jnp.where(x == thr, 1.0, 0.0).astype(BF16)
        off = jnp.zeros((t_new, LANE), F32)
        for gg, j in blocks:
            x = tile(gg, j)
            eqb = ties(x)
            sel = (x > thr) | ((x == thr) & (off + _dot(eqb, tri) <= need))
            mpast_ref[0, :, gg * gw + j * LANE:gg * gw + (j + 1) * LANE] = jnp.where(sel, 0.0, NEG_INF)
            off = off + jnp.sum(eqb.astype(F32), axis=-1, keepdims=True)
        x = scn[...]
        sel = ((x > thr) | ((x == thr) & (off + _dot(ties(x), tri) <= need))) & causal_n
        mnew_ref[0] = jnp.where(sel, 0.0, NEG_INF)


def _sample_select(page_table, qi_hm, tail, cache_idx_t, t_new):
    bsz, n_pages = page_table.shape
    n_groups = n_pages // SEL_PAGES
    past = n_pages * PAGE
    topk = min(TOPK_MAX, (past + t_new) // 4)

    def page_spec(j):
        return pl.BlockSpec((1, IDX_DIM, PAGE), lambda b, g, pt: (pt[b, g * SEL_PAGES + j], 0, 0))

    grid_spec = pltpu.PrefetchScalarGridSpec(
        num_scalar_prefetch=1,
        grid=(bsz, n_groups),
        in_specs=[pl.BlockSpec((IDX_HEADS, t_new, 64), lambda b, g, pt: (0, b, 0)),
                  pl.BlockSpec((t_new, 128), lambda b, g, pt: (b, 0))]
                 + [page_spec(j) for j in range(SEL_PAGES)],
        out_specs=[pl.BlockSpec((1, t_new, past), lambda b, g, pt: (b, 0, 0)),
                   pl.BlockSpec((1, t_new, LANE), lambda b, g, pt: (b, 0, 0))],
        scratch_shapes=[pltpu.VMEM((n_groups, t_new, SEL_PAGES * PAGE), F32),
                        pltpu.VMEM((t_new, LANE), F32)])
    return pl.pallas_call(
        functools.partial(_sample_select_kernel, n_groups=n_groups, t_new=t_new, topk=topk),
        grid_spec=grid_spec,
        out_shape=[jax.ShapeDtypeStruct((bsz, t_new, past), F32),
                   jax.ShapeDtypeStruct((bsz, t_new, LANE), F32)],
        compiler_params=_cparams(2),
        name="sample_select",
    )(page_table, qi_hm, tail, *([cache_idx_t] * SEL_PAGES))


def _sample_attn_kernel(pt_ref, btab_ref, rb_ref, q_ref, knew_ref, vnew_ref, mpast_ref, mnew_ref, *rest,
                        n_steps, t_new):
    kpages = rest[:ATT_PAGES]
    vpages = rest[ATT_PAGES:2 * ATT_PAGES]
    o_ref, qbd, dq, m_s, l_s, acc_s = rest[2 * ATT_PAGES:]
    g = pl.program_id(1)
    hq = N_HEADS_A * t_new
    gw = ATT_PAGES * PAGE
    head_of_row = lax.broadcasted_iota(I32, (hq, A_WIDTH), 0) // t_new
    head_of_col = lax.broadcasted_iota(I32, (hq, A_WIDTH), 1) // HEAD_DIM
    own = head_of_row == head_of_col

    @pl.when((pl.program_id(0) == 0) & (g == 0))
    def _():
        bt = btab_ref[0:t_new, :]
        for hd in range(N_HEADS_A):
            dq[hd * t_new:(hd + 1) * t_new, :] = _bias_delta(bt, rb_ref, hd)

    @pl.when(g == 0)
    def _():
        qt = jnp.concatenate([q_ref[...].astype(F32)] * N_HEADS_A, axis=0)
        qbd[...] = jnp.where(own, qt, 0.0).astype(BF16)
        m_s[...] = jnp.full((hq, 1), NEG_INF, F32)
        l_s[...] = jnp.zeros((hq, 1), F32)
        acc_s[...] = jnp.zeros((hq, A_WIDTH), F32)

    def online(x, vb_t):
        m_old = m_s[...]
        m_new = jnp.maximum(m_old, jnp.max(x, axis=-1, keepdims=True))
        m_fin = jnp.where(m_new == NEG_INF, 0.0, m_new)
        alpha = jnp.exp(m_old - m_fin)
        p = jnp.exp(x - m_fin)
        l_s[...] = alpha * l_s[...] + jnp.sum(p, axis=-1, keepdims=True)
        acc_s[...] = alpha * acc_s[...] + _dot_t(p.astype(BF16), vb_t)
        m_s[...] = m_new

    kb_t = jnp.concatenate([p[0] for p in kpages], axis=1).astype(BF16)
    vb_t = jnp.concatenate([p[0] for p in vpages], axis=1).astype(BF16)
    x = _dot(qbd[...], kb_t) + jnp.concatenate([mpast_ref[0]] * N_HEADS_A, axis=0)
    lane = lax.broadcasted_iota(I32, (hq, gw), 1)
    near = jnp.concatenate([jnp.zeros((hq, gw - LANE), F32), dq[:, :LANE]], axis=1)
    x = x + jnp.where(g == n_steps - 1, near, 0.0)

    online(x, vb_t)

    @pl.when(g == n_steps - 1)
    def _():
        pad = jnp.zeros((LANE - t_new, A_WIDTH), F32)
        kn = jnp.concatenate([knew_ref[...], pad], axis=0).astype(BF16)
        vn = jnp.concatenate([vnew_ref[...], pad], axis=0)
        xn = _dot_t(qbd[...], kn) + jnp.concatenate([mnew_ref[0]] * N_HEADS_A, axis=0) + dq[:, LANE:]
        online(xn, vn.T.astype(BF16))
        res = jnp.where(own, acc_s[...] / l_s[...], 0.0)
        out = res[0:t_new, :]
        for hd in range(1, N_HEADS_A):
            out = out + res[hd * t_new:(hd + 1) * t_new, :]
        o_ref[...] = out


def _sample_attn(page_table, btab, rel_bias, q_rows, k_new, v_new, mpast, mnew, cache_kt, cache_vt, t_new):
    bsz, n_pages = page_table.shape
    n_steps = n_pages // ATT_PAGES
    hq = N_HEADS_A * t_new

    def page_spec(j):
        return pl.BlockSpec((1, A_WIDTH, PAGE), lambda b, g, pt: (pt[b, g * ATT_PAGES + j], 0, 0))

    grid_spec = pltpu.PrefetchScalarGridSpec(
        num_scalar_prefetch=1,
        grid=(bsz, n_steps),
        in_specs=[pl.BlockSpec((QB, 2 * QB), lambda b, g, pt: (0, 0)),
                  pl.BlockSpec(memory_space=pltpu.SMEM),
                  pl.BlockSpec((t_new, A_WIDTH), lambda b, g, pt: (b, 0)),
                  pl.BlockSpec((t_new, A_WIDTH), lambda b, g, pt: (b, 0)),
                  pl.BlockSpec((t_new, A_WIDTH), lambda b, g, pt: (b, 0)),
                  pl.BlockSpec((1, t_new, ATT_PAGES * PAGE), lambda b, g, pt: (b, 0, g)),
                  pl.BlockSpec((1, t_new, LANE), lambda b, g, pt: (b, 0, 0))]
                 + [page_spec(j) for j in range(ATT_PAGES)] * 2,
        out_specs=pl.BlockSpec((t_new, A_WIDTH), lambda b, g, pt: (b, 0)),
        scratch_shapes=[pltpu.VMEM((hq, A_WIDTH), BF16),
                        pltpu.VMEM((hq, 2 * QB), F32),
                        pltpu.VMEM((hq, 1), F32),
                        pltpu.VMEM((hq, 1), F32),
                        pltpu.VMEM((hq, A_WIDTH), F32)])
    return pl.pallas_call(
        functools.partial(_sample_attn_kernel, n_steps=n_steps, t_new=t_new),
        grid_spec=grid_spec,
        out_shape=jax.ShapeDtypeStruct((bsz * t_new, A_WIDTH), F32),
        compiler_params=_cparams(2),
        name="sample_attn",
    )(page_table, btab, rel_bias, q_rows, k_new, v_new, mpast, mnew,
      *([cache_kt] * ATT_PAGES), *([cache_vt] * ATT_PAGES))


def _head_block_ones():
    r = lax.broadcasted_iota(I32, (B_WIDTH, B_WIDTH), 0) // HEAD_B
    c = lax.broadcasted_iota(I32, (B_WIDTH, B_WIDTH), 1) // HEAD_B
    return jnp.where(r == c, 1.0, 0.0).astype(BF16)


def _rwkv_prep_kernel(rw_ref, prev8_ref, prow_ref, mu_ref, w0_ref, a0_ref, kk_ref, ka_ref, rk_ref,
                      w2_ref, a2_ref, g2_ref,
                      r_ref, w_ref, k_ref, v_ref, na_ref, bb_ref, bonus_ref, g_ref, *, tt):
    t = pl.program_id(1)
    cur = rw_ref[0]
    first_prev = jnp.where(t == 0, prow_ref[0], prev8_ref[0, 7:8, :])
    rows = lax.broadcasted_iota(I32, cur.shape, 0)
    prev = jnp.where(rows == 0, first_prev, pltpu.roll(cur, 1, 0))
    mixed = cur + (prev - cur) * mu_ref[...]
    r = mixed[:, 0:512]
    k = mixed[:, 512:1024]
    v = mixed[:, 1024:1536]
    wd = mixed[:, 1536:1600]
    ad = mixed[:, 1600:1664]
    gd = mixed[:, 1664:1792]
    z = w0_ref[...] + _dot(jnp.tanh(wd).astype(BF16), w2_ref[...])
    u = -z
    softplus = jnp.maximum(u, 0.0) + jnp.log1p(jnp.exp(-jnp.abs(u)))
    decay = jnp.exp(-jnp.exp(-softplus - 0.5))
    a = jax.nn.sigmoid(a0_ref[...] + _dot(ad.astype(BF16), a2_ref[...]))
    g = _dot(jax.nn.sigmoid(gd).astype(BF16), g2_ref[...])
    bd = _head_block_ones()
    kk = k * kk_ref[...]
    nrm = jnp.sqrt(_split_dot(kk * kk, bd))
    kk = kk / jnp.maximum(nrm, 1e-12)
    k2 = k * (1.0 + (a - 1.0) * ka_ref[...])
    r_ref[0] = r
    w_ref[0] = decay
    k_ref[0] = k2
    v_ref[0] = v
    na_ref[0] = -kk
    bb_ref[0] = kk * a
    bonus_ref[0] = _split_dot(r * k2 * rk_ref[...], bd) * v
    g_ref[0] = g


def _rwkv_prep(rw3, prow, p, tt):
    bsz, t_len, _ = rw3.shape
    vec = lambda c: pl.BlockSpec((1, c), lambda b, t: (0, 0))
    full = lambda r, c: pl.BlockSpec((r, c), lambda b, t: (0, 0))
    tile = pl.BlockSpec((1, tt, B_WIDTH), lambda b, t: (b, t, 0))
    return pl.pallas_call(
        functools.partial(_rwkv_prep_kernel, tt=tt),
        grid=(bsz, t_len // tt),
        in_specs=[pl.BlockSpec((1, tt, RWKV_COLS), lambda b, t: (b, t, 0)),
                  pl.BlockSpec((1, 8, RWKV_COLS), lambda b, t: (b, jnp.maximum(t * (tt // 8) - 1, 0), 0)),
                  pl.BlockSpec((1, 1, RWKV_COLS), lambda b, t: (b, 0, 0)),
                  vec(RWKV_COLS), vec(B_WIDTH), vec(B_WIDTH), vec(B_WIDTH), vec(B_WIDTH), vec(B_WIDTH),
                  full(D_DECAY, B_WIDTH), full(D_AAA, B_WIDTH), full(D_GATE, B_WIDTH)],
        out_specs=[tile] * 8,
        out_shape=[jax.ShapeDtypeStruct((bsz, t_len, B_WIDTH), F32)] * 8,
        compiler_params=_cparams(2),
        name="rwkv_prep",
    )(rw3, rw3, prow.reshape(bsz, 1, RWKV_COLS), p["mu"], p["w0"], p["a0"], p["k_k"], p["k_a"], p["r_k"],
      p["w2"], p["a2"], p["g2"])


SCAN_BLOCK = 2


def _rwkv_scan_kernel(r_ref, w_ref, k_ref, v_ref, na_ref, bb_ref, s0_ref, y_ref, st_ref, *, bb_n, tc):
    @pl.when(pl.program_id(1) == 0)
    def _():
        st_ref[...] = s0_ref[...]

    lane = lax.broadcasted_iota(I32, (HEAD_B, LANE), 1)
    rowi = lax.broadcasted_iota(I32, (HEAD_B, LANE), 0)
    diag = (lane & (HEAD_B - 1)) == rowi
    step_lane = lane & (HEAD_B - 1)
    bj = lax.broadcasted_iota(I32, (2 * LANE, LANE), 0)
    bl = lax.broadcasted_iota(I32, (2 * LANE, LANE), 1)
    bd2 = jnp.where((bj & (LANE - 1)) // HEAD_B == bl // HEAD_B, 1.0, 0.0).astype(BF16)
    bd1 = bd2[0:LANE]

    def hi_lo(x):
        hi = x.astype(BF16)
        return hi, (x - hi.astype(F32)).astype(BF16)

    chains = [(bi, pr) for bi in range(bb_n) for pr in range(N_HEADS_B // 2)]

    def tile(t8, carry):
        t0 = pl.multiple_of(t8 * 8, 8)
        ins = {c: tuple(ref[c[0], pl.ds(t0, 8), c[1] * LANE:(c[1] + 1) * LANE]
                        for ref in (r_ref, w_ref, k_ref, v_ref, na_ref, bb_ref)) for c in chains}
        state = {c: st_ref[c[0], c[1]] for c in chains}
        ycols = {c: jnp.zeros((HEAD_B, LANE), F32) for c in chains}
        nch = len(chains)
        piece = lambda x, j: x[j * HEAD_B:(j + 1) * HEAD_B]

        def shift_up(x, d):
            return pltpu.roll(x, 8 - d, 0)

        prods = []
        for d in range(1, SCAN_BLOCK):
            for n in (5, 2):
                for c in chains:
                    x = ins[c][n] * shift_up(ins[c][4], d)
                    for u in range(1, d):
                        x = x * shift_up(ins[c][1], u)
                    prods.append(x)
        if prods:
            dots = _dot(jnp.concatenate(hi_lo(jnp.concatenate(prods, axis=0)), axis=1), bd2)

        def coef(d, which, j, l):
            row = (((d - 1) * 2 + which) * nch + j) * 8 + l
            return dots[row:row + 1, :]

        for i0 in range(0, 8, SCAN_BLOCK):
            steps = range(i0, i0 + SCAN_BLOCK)
            row = lambda c, n, i: ins[c][n][i:i + 1, :]
            p_all = []
            for i in steps:
                for c in chains:
                    at = row(c, 4, i)
                    for u in range(i0, i):
                        at = at * row(c, 1, u)
                    p_all.append(jnp.concatenate(hi_lo(state[c] * at), axis=1))
            sab = _dot(jnp.concatenate(p_all, axis=0), bd2)
            z_all = [jnp.where(diag, row(c, 3, i), 0.0).astype(BF16) for i in steps for c in chains]
            vcs = _dot(jnp.concatenate(z_all, axis=0), bd1)
            y_all = []
            for j, c in enumerate(chains):
                s = state[c]
                sas = []
                for m, i in enumerate(steps):
                    sa = piece(sab, m * nch + j)
                    for ml in range(m):
                        sa = sa + sas[ml] * coef(m - ml, 0, j, i0 + ml) \
                            + piece(vcs, ml * nch + j) * coef(m - ml, 1, j, i0 + ml)
                    sas.append(sa)
                    s = s * row(c, 1, i) + sa * row(c, 5, i) + piece(vcs, m * nch + j) * row(c, 2, i)
                    y_all.append((s * row(c, 0, i)).astype(BF16))
                state[c] = s
            yb_all = _dot(jnp.concatenate(y_all, axis=0), bd1)
            for j, c in enumerate(chains):
                for m, i in enumerate(steps):
                    ycols[c] = jnp.where(step_lane == i, piece(yb_all, j * SCAN_BLOCK + m), ycols[c])
        for c in chains:
            st_ref[c[0], c[1]] = state[c]
            yt = ycols[c].T
            y_ref[c[0], pl.ds(t0, 8), c[1] * LANE:c[1] * LANE + HEAD_B] = yt[0:8]
            y_ref[c[0], pl.ds(t0, 8), c[1] * LANE + HEAD_B:(c[1] + 1) * LANE] = yt[HEAD_B:HEAD_B + 8]
        return carry

    lax.fori_loop(0, tc // 8, tile, 0)


def _pair_pack(state):
    b = state.shape[0]
    s = state.reshape(b, N_HEADS_B // 2, 2, HEAD_B, HEAD_B)
    return jnp.transpose(s, (0, 1, 3, 2, 4)).reshape(b, N_HEADS_B // 2, HEAD_B, 2 * HEAD_B)


def _pair_unpack(state):
    b = state.shape[0]
    s = state.reshape(b, N_HEADS_B // 2, HEAD_B, 2, HEAD_B)
    return jnp.transpose(s, (0, 1, 3, 2, 4)).reshape(b, N_HEADS_B, HEAD_B, HEAD_B)


def _rwkv_scan(r, w, k, v, na, bb, state0, bb_n, tc):
    bsz, t_len, _ = r.shape
    tile = pl.BlockSpec((bb_n, tc, B_WIDTH), lambda b, t: (b, t, 0))
    st = pl.BlockSpec((bb_n, N_HEADS_B // 2, HEAD_B, 2 * HEAD_B), lambda b, t: (b, 0, 0, 0))
    y, state = pl.pallas_call(
        functools.partial(_rwkv_scan_kernel, bb_n=bb_n, tc=tc),
        grid=(bsz // bb_n, t_len // tc),
        in_specs=[tile] * 6 + [st],
        out_specs=[tile, st],
        out_shape=[jax.ShapeDtypeStruct((bsz, t_len, B_WIDTH), F32),
                   jax.ShapeDtypeStruct((bsz, N_HEADS_B // 2, HEAD_B, 2 * HEAD_B), F32)],
        compiler_params=_cparams(2),
        name="rwkv_scan",
    )(r, w, k, v, na, bb, _pair_pack(state0))
    return y, _pair_unpack(state)


def _tail_kernel(x_ref, att_ref, y_ref, bonus_ref, g_ref, gate_ref, lnw_ref, lnb_ref, nm_ref, nf_ref,
                 wa_ref, wb_ref, wo_ref, w1_ref, w2_ref, o_ref):
    bd = _head_block_ones()
    y = y_ref[...]
    mean = _split_dot(y, bd) * (1.0 / HEAD_B)
    yc = y - mean
    var = _split_dot(yc * yc, bd) * (1.0 / HEAD_B)
    yn = yc * lax.rsqrt(var + GN_EPS) * lnw_ref[...] + lnb_ref[...]
    rwkv = (yn + bonus_ref[...]) * g_ref[...]
    ga = gate_ref[:, 0:D_MODEL]
    gb = gate_ref[:, D_MODEL:2 * D_MODEL]
    merged = (jax.nn.sigmoid(ga) * _dot(att_ref[...].astype(BF16), wa_ref[...])
              + jax.nn.sigmoid(gb) * _dot(rwkv.astype(BF16), wb_ref[...]))
    x1 = x_ref[...] + _dot(merged.astype(BF16), wo_ref[...])
    hm = _rms(x1, nm_ref[...]).astype(BF16)
    up = jnp.maximum(_dot(hm, w1_ref[...]), 0.0)
    x2 = x1 + _dot((up * up).astype(BF16), w2_ref[...])
    o_ref[...] = _rms(x2, nf_ref[...])


def _tail(x2d, att, y, bonus, g, gates, p, tm):
    n = x2d.shape[0]
    row = lambda i: (i, 0)
    vec = lambda c: pl.BlockSpec((1, c), lambda i: (0, 0))
    res = lambda r, c: _resident((r, c), lambda i: (0, 0))
    return pl.pallas_call(
        _tail_kernel,
        grid=(n // tm,),
        in_specs=[pl.BlockSpec((tm, D_MODEL), row),
                  pl.BlockSpec((tm, A_WIDTH), row), pl.BlockSpec((tm, B_WIDTH), row),
                  pl.BlockSpec((tm, B_WIDTH), row), pl.BlockSpec((tm, B_WIDTH), row),
                  pl.BlockSpec((tm, 2 * D_MODEL), row),
                  vec(B_WIDTH), vec(B_WIDTH), vec(D_MODEL), vec(D_MODEL),
                  res(A_WIDTH, D_MODEL), res(B_WIDTH, D_MODEL), res(D_MODEL, D_MODEL),
                  res(D_MODEL, D_FF), res(D_FF, D_MODEL)],
        out_specs=pl.BlockSpec((tm, D_MODEL), row),
        out_shape=jax.ShapeDtypeStruct((n, D_MODEL), F32),
        compiler_params=_cparams(1),
        name="tail",
    )(x2d, att, y, bonus, g, gates, p["ln_w"], p["ln_b"], p["norm_mlp"], p["norm_final"],
      p["wa"], p["wb"], p["wo"], p["w1"], p["w2m"])


def _prepare_params(norm_mix, w_in, rwkv_mu, rwkv_w0, rwkv_w2, rwkv_a0, rwkv_a2, rwkv_g2, rwkv_k_k, rwkv_k_a,
                    rwkv_r_k, rwkv_ln_w, rwkv_ln_b, w_branch_a, w_branch_b, w_out, norm_mlp, w_mlp_in,
                    w_mlp_out, norm_final):
    wb16 = w_in.astype(BF16)
    o_idx = 3 * A_WIDTH
    o_kidx = o_idx + IDX_HEADS * IDX_DIM
    o_widx = o_kidx + IDX_DIM
    pad = jnp.zeros((D_MODEL, 128 - IDX_DIM - IDX_HEADS), BF16)
    w_attn = jnp.concatenate([wb16[:, :o_kidx], wb16[:, o_kidx:o_widx + IDX_HEADS], pad], axis=1)
    o = RWKV_OFF
    def regroup(a):
        return jnp.concatenate([a[..., 0:512], a[..., 576:1600], a[..., 512:576], a[..., 1600:1792]], axis=-1)
    w_rw = regroup(wb16[:, o:o + RWKV_COLS])
    w_rest = jnp.concatenate([w_rw, wb16[:, o + RWKV_COLS:]], axis=1)
    r1 = lambda a: a.reshape(1, -1).astype(F32)
    return dict(
        norm_mix=r1(norm_mix), w_attn=w_attn, w_rest=w_rest, w_rw=w_rw,
        mu=r1(regroup(rwkv_mu)), w0=r1(rwkv_w0), a0=r1(rwkv_a0), k_k=r1(rwkv_k_k), k_a=r1(rwkv_k_a),
        r_k=r1(rwkv_r_k), w2=rwkv_w2.astype(BF16), a2=rwkv_a2.astype(BF16), g2=rwkv_g2.astype(BF16),
        ln_w=r1(rwkv_ln_w), ln_b=r1(rwkv_ln_b), norm_mlp=r1(norm_mlp), norm_final=r1(norm_final),
        wa=w_branch_a.astype(BF16), wb=w_branch_b.astype(BF16), wo=w_out.astype(BF16),
        w1=w_mlp_in.astype(BF16), w2m=w_mlp_out.astype(BF16))


def _layer(x, shift_rows, wkv0, attend, p, tm, tm_tail, scan_tc):
    bsz, t_len, _ = x.shape
    n = bsz * t_len
    x2d = x.reshape(n, D_MODEL)
    hl_rows = 8 if t_len >= tm else tm
    names = ("q_hm", "k", "v", "qt", "kb", "vt", "qi_hm", "qit", "tail", "tailb", "wt", "hlast")
    long_seq = t_len >= tm
    pr = dict(zip(names, _proj_attn(x2d, p["norm_mix"], p["w_attn"], tm, hl_rows, t_len // tm if long_seq else 0)))
    tail, hlast = pr["tail"], pr["hlast"]
    if long_seq:
        heads = lambda a: jnp.transpose(a.reshape(bsz, N_HEADS_A, HEAD_DIM, t_len), (0, 3, 1, 2))
    else:
        heads = lambda a: a.reshape(bsz, t_len, N_HEADS_A, HEAD_DIM)
    rw, gates = _proj_rest(x2d, p["norm_mix"], p["w_rest"], tm)
    att = attend(pr)
    prow = _shift_proj(shift_rows, p["w_rw"])
    r, w, k2, v2, na, bb, bonus, g = _rwkv_prep(rw.reshape(bsz, t_len, RWKV_COLS), prow, p, min(256, t_len))
    y, wkv = _rwkv_scan(r, w, k2, v2, na, bb, wkv0, 2, scan_tc)
    flat = lambda a: a.reshape(n, B_WIDTH)
    out = _tail(x2d, att, flat(y), flat(bonus), flat(g), gates, p, tm_tail)
    if long_seq:
        shift = hlast.reshape(bsz, t_len // tm, 8, D_MODEL)[:, -1, 7]
    else:
        shift = hlast.reshape(bsz, t_len, D_MODEL)[:, -1]
    return (out.reshape(bsz, t_len, D_MODEL), heads(pr["k"]), heads(pr["v"]),
            tail[:, :IDX_DIM].reshape(bsz, t_len, IDX_DIM), wkv, shift)


def kernel(x_prompt, x_sample, cache_k, cache_v, cache_idx_k, state_wkv, state_shift, page_table, rel_bias, norm_mix, w_in, rwkv_mu, rwkv_w0, rwkv_w2, rwkv_a0, rwkv_a2, rwkv_g2, rwkv_k_k, rwkv_k_a, rwkv_r_k, rwkv_ln_w, rwkv_ln_b, w_branch_a, w_branch_b, w_out, norm_mlp, w_mlp_in, w_mlp_out, norm_final):
    p = _prepare_params(norm_mix, w_in, rwkv_mu, rwkv_w0, rwkv_w2, rwkv_a0, rwkv_a2, rwkv_g2, rwkv_k_k,
                        rwkv_k_a, rwkv_r_k, rwkv_ln_w, rwkv_ln_b, w_branch_a, w_branch_b, w_out, norm_mlp,
                        w_mlp_in, w_mlp_out, norm_final)
    btab = jnp.asarray(_window_buckets())
    btab_t = jnp.asarray(_window_buckets_t())
    b_p, s_p, _ = x_prompt.shape
    b_s, t_s, _ = x_sample.shape
    n_pool = cache_k.shape[0]

    def prompt_attend(pr):
        return _prompt_attn(btab_t, rel_bias, pr["qt"], pr["qit"], pr["wt"], pr["tailb"], pr["kb"], pr["vt"],
                            b_p, s_p)

    def sample_attend(pr):
        key_minor = lambda c: jnp.transpose(c.reshape(n_pool, PAGE, -1), (0, 2, 1))
        mpast, mnew = _sample_select(page_table, pr["qi_hm"], pr["tail"], key_minor(cache_idx_k), t_s)
        q_rows = jnp.transpose(pr["q_hm"], (1, 0, 2)).reshape(b_s * t_s, A_WIDTH)
        return _sample_attn(page_table, btab, rel_bias, q_rows, pr["k"], pr["v"], mpast, mnew,
                            key_minor(cache_k), key_minor(cache_v), t_s)

    zero_shift = jnp.zeros((b_p, D_MODEL), F32)
    zero_wkv = jnp.zeros((b_p, N_HEADS_B, HEAD_B, HEAD_B), F32)
    y_p, k_p, v_p, ik_p, wkv_p, sh_p = _layer(x_prompt, zero_shift, zero_wkv, prompt_attend, p, KC, 256, 256)
    y_s, k_s, v_s, ik_s, wkv_s, sh_s = _layer(x_sample, state_shift, state_wkv, sample_attend, p,
                                              b_s * t_s, b_s * t_s, t_s)
    return (y_p, y_s, k_p, v_p, ik_p, wkv_p, sh_p, k_s, v_s, ik_s, wkv_s, sh_s)
```

```python
import functools
import math

import numpy as np
import jax
import jax.numpy as jnp
from jax import lax
from jax.experimental import pallas as pl
from jax.experimental.pallas import tpu as pltpu

F32 = jnp.float32
BF16 = jnp.bfloat16
I32 = jnp.int32

D_MODEL = 1024
PAGE = 128
HEAD_DIM = 64
A_WIDTH = 512
N_HEADS_A = 8
IDX_HEADS = 4
IDX_DIM = 64
IDX_SCALE = (IDX_HEADS * IDX_DIM) ** -0.5
ATT_SCALE = HEAD_DIM ** -0.5
TOPK_MAX = 256
REL_BUCKETS = 32
REL_MAX_DIST = 128
HEAD_B = 64
B_WIDTH = 512
N_HEADS_B = 8
D_DECAY = 64
D_AAA = 64
D_GATE = 128
GN_EPS = 64e-5
D_FF = 4096
RMS_EPS = 1e-6
RWKV_COLS = 3 * B_WIDTH + D_DECAY + D_AAA + D_GATE
RWKV_OFF = 3 * A_WIDTH + IDX_HEADS * IDX_DIM + IDX_DIM + IDX_HEADS

LANE = 128
QB = 128
PQB = 256
KC = 512
ATT_SUB = KC
NEG_INF = float("-inf")
INT_MIN = -(2 ** 31)
VMEM_LIMIT = 56 * 1024 * 1024


def _cparams(n_axes):
    return pltpu.CompilerParams(dimension_semantics=("arbitrary",) * n_axes,
                                vmem_limit_bytes=VMEM_LIMIT)


def _resident(shape, index_map):
    return pl.BlockSpec(shape, index_map, pipeline_mode=pl.Buffered(1))


def _bucket_table(dist):
    dist = np.asarray(dist, np.int64)
    max_exact = REL_BUCKETS // 2
    d_f = np.maximum(dist, max_exact).astype(np.float32)
    large = max_exact + (np.log(d_f / np.float32(max_exact)) / np.float32(math.log(REL_MAX_DIST / max_exact))
                         * np.float32(REL_BUCKETS - max_exact)).astype(np.int32)
    large = np.minimum(large, REL_BUCKETS - 1)
    return np.where(dist < max_exact, dist, large).astype(np.int32)


def _window_buckets():
    t = np.arange(QB)[:, None]
    c = np.arange(2 * QB)[None, :]
    return _bucket_table(np.maximum(QB + t - c, 0))


def _window_buckets_t():
    k = np.arange(2 * PQB)[:, None]
    q = np.arange(PQB)[None, :]
    return _bucket_table(np.maximum(PQB + q - k, 0))


def _rms(x, g):
    return x * lax.rsqrt(jnp.mean(x * x, axis=-1, keepdims=True) + RMS_EPS) * g


def _dot(a, b):
    return jnp.dot(a, b, preferred_element_type=F32)


def _dot_t(a, b):
    return lax.dot_general(a, b, (((1,), (1,)), ((), ())), preferred_element_type=F32)


def _split_dot(x, m_bf16):
    hi = x.astype(BF16)
    lo = (x - hi.astype(F32)).astype(BF16)
    return _dot(hi, m_bf16) + _dot(lo, m_bf16)


def _kth_largest(count_ge, k, shape):
    def bit_step(it, t_u):
        cand_u = t_u | jnp.left_shift(jnp.int32(1), 31 - it)
        return jnp.where(count_ge(_ordered_f32(cand_u)) >= float(k), cand_u, t_u)
    return _ordered_f32(lax.fori_loop(0, 32, bit_step, jnp.zeros(shape, I32)))


def _ordered_f32(u):
    s = u ^ INT_MIN
    bits = jnp.where(s < 0, INT_MIN | (-s), s)
    return pltpu.bitcast(bits, F32)


def _proj_attn_kernel(x_ref, g_ref, w_ref, q_ref, k_ref, v_ref, qt_ref, kb_ref, vt_ref, qi_ref, qit_ref,
                      tail_ref, tailb_ref, wt_ref, hl_ref, *, tm, hl_rows, kv_feature_major):
    h = _rms(x_ref[...], g_ref[...])
    hl_ref[...] = h[tm - hl_rows:, :]
    hb = h.astype(BF16)
    q = _dot(hb, w_ref[:, 0:512]) * ATT_SCALE
    for hd in range(N_HEADS_A):
        q_ref[hd] = q[:, hd * 64:(hd + 1) * 64].astype(BF16)
    qt_ref[0] = q.T.astype(BF16)
    k = _dot(hb, w_ref[:, 512:1024])
    kb_ref[...] = k.astype(BF16)
    v = _dot(hb, w_ref[:, 1024:1536])
    v_t = v.T
    vt_ref[0] = v_t.astype(BF16)
    if kv_feature_major:
        k_ref[0] = k.T
        v_ref[0] = v_t
    else:
        k_ref[...] = k
        v_ref[...] = v
    qi = _dot(hb, w_ref[:, 1536:1792])
    for hd in range(IDX_HEADS):
        qi_ref[hd] = qi[:, hd * 64:(hd + 1) * 64].astype(BF16)
    qit_ref[0] = qi.T.astype(BF16)
    tail = _dot(hb, w_ref[:, 1792:1920])
    tail_ref[...] = tail
    tailb_ref[...] = tail.astype(BF16)
    wt_ref[0] = tail.T[IDX_DIM:IDX_DIM + 8, :]


def _proj_attn(x2d, g, w_attn, tm, hl_rows, seq_tiles):
    n = x2d.shape[0]
    nt = n // tm
    row = lambda i: (i, 0)
    if seq_tiles:
        kv_spec = pl.BlockSpec((1, 512, tm), lambda i: (i // seq_tiles, 0, i % seq_tiles))
        kv_shape = jax.ShapeDtypeStruct((nt // seq_tiles, 512, seq_tiles * tm), F32)
    else:
        kv_spec = pl.BlockSpec((tm, 512), row)
        kv_shape = jax.ShapeDtypeStruct((n, 512), F32)
    return pl.pallas_call(
        functools.partial(_proj_attn_kernel, tm=tm, hl_rows=hl_rows, kv_feature_major=bool(seq_tiles)),
        grid=(nt,),
        in_specs=[pl.BlockSpec((tm, D_MODEL), row),
                  pl.BlockSpec((1, D_MODEL), lambda i: (0, 0)),
                  _resident((D_MODEL, 1920), lambda i: (0, 0))],
        out_specs=[pl.BlockSpec((N_HEADS_A, tm, 64), lambda i: (0, i, 0)),
                   kv_spec,
                   kv_spec,
                   pl.BlockSpec((1, 512, tm), lambda i: (i, 0, 0)),
                   pl.BlockSpec((tm, 512), row),
                   pl.BlockSpec((1, 512, tm), lambda i: (i, 0, 0)),
                   pl.BlockSpec((IDX_HEADS, tm, 64), lambda i: (0, i, 0)),
                   pl.BlockSpec((1, 256, tm), lambda i: (i, 0, 0)),
                   pl.BlockSpec((tm, 128), row),
                   pl.BlockSpec((tm, 128), row),
                   pl.BlockSpec((1, 8, tm), lambda i: (i, 0, 0)),
                   pl.BlockSpec((hl_rows, D_MODEL), row)],
        out_shape=[jax.ShapeDtypeStruct((N_HEADS_A, n, 64), BF16),
                   kv_shape,
                   kv_shape,
                   jax.ShapeDtypeStruct((nt, 512, tm), BF16),
                   jax.ShapeDtypeStruct((n, 512), BF16),
                   jax.ShapeDtypeStruct((nt, 512, tm), BF16),
                   jax.ShapeDtypeStruct((IDX_HEADS, n, 64), BF16),
                   jax.ShapeDtypeStruct((nt, 256, tm), BF16),
                   jax.ShapeDtypeStruct((n, 128), F32),
                   jax.ShapeDtypeStruct((n, 128), BF16),
                   jax.ShapeDtypeStruct((nt, 8, tm), F32),
                   jax.ShapeDtypeStruct((nt * hl_rows, D_MODEL), F32)],
        compiler_params=_cparams(1),
        name="proj_attn",
    )(x2d, g, w_attn)


def _proj_rest_kernel(x_ref, g_ref, w_ref, rw_ref, gate_ref):
    hb = _rms(x_ref[...], g_ref[...]).astype(BF16)
    rw_ref[...] = _dot(hb, w_ref[:, 0:RWKV_COLS])
    gate_ref[...] = _dot(hb, w_ref[:, RWKV_COLS:RWKV_COLS + 2 * D_MODEL])


def _proj_rest(x2d, g, w_rest, tm):
    n = x2d.shape[0]
    row = lambda i: (i, 0)
    return pl.pallas_call(
        _proj_rest_kernel,
        grid=(n // tm,),
        in_specs=[pl.BlockSpec((tm, D_MODEL), row),
                  pl.BlockSpec((1, D_MODEL), lambda i: (0, 0)),
                  _resident((D_MODEL, RWKV_COLS + 2 * D_MODEL), lambda i: (0, 0))],
        out_specs=[pl.BlockSpec((tm, RWKV_COLS), row),
                   pl.BlockSpec((tm, 2 * D_MODEL), row)],
        out_shape=[jax.ShapeDtypeStruct((n, RWKV_COLS), F32),
                   jax.ShapeDtypeStruct((n, 2 * D_MODEL), F32)],
        compiler_params=_cparams(1),
        name="proj_rest",
    )(x2d, g, w_rest)


def _shift_proj_kernel(s_ref, w_ref, o_ref):
    o_ref[...] = _dot(s_ref[...].astype(BF16), w_ref[...])


def _shift_proj(shift_rows, w_rw):
    b = shift_rows.shape[0]
    return pl.pallas_call(
        _shift_proj_kernel,
        out_shape=jax.ShapeDtypeStruct((b, RWKV_COLS), F32),
        compiler_params=pltpu.CompilerParams(vmem_limit_bytes=VMEM_LIMIT),
        name="shift_proj",
    )(shift_rows, w_rw)


def _bias_delta(btab, rb_ref, hd):
    out = jnp.zeros(btab.shape, F32)
    far = rb_ref[REL_BUCKETS - 1, hd]
    for j in range(REL_BUCKETS - 1):
        out = jnp.where(btab == j, rb_ref[j, hd] - far, out)
    return out


def _tri_ones():
    r = lax.broadcasted_iota(I32, (LANE, LANE), 0)
    c = lax.broadcasted_iota(I32, (LANE, LANE), 1)
    return jnp.where(r <= c, 1.0, 0.0).astype(BF16), jnp.ones((LANE, LANE), BF16)


def _prompt_attn_kernel(btab_ref, rb_ref, qt_ref, qit_ref, wt_ref, tailb_ref, kb_ref, vt_ref, o_ref,
                        sc, dt, qpad, qipad, m_s, l_s, acc_s, lg, *, topk):
    b = pl.program_id(0)
    i = pl.program_id(1)
    nc = (i * PQB + PQB - 1) // KC + 1

    @pl.when((b == 0) & (i == 0))
    def _():
        bt = btab_ref[...]
        for hd in range(N_HEADS_A):
            dt[hd] = _bias_delta(bt, rb_ref, hd)

    q_pos = lax.broadcasted_iota(I32, (KC, PQB), 1) + i * PQB
    k_off = lax.broadcasted_iota(I32, (KC, PQB), 0)

    def keys(ref, c, cols):
        return ref[pl.ds(pl.multiple_of(c * KC, KC), KC), cols]

    def fold8(x):
        rows = x.shape[0]
        while rows > 8:
            rows //= 2
            x = x[:rows, :] + x[rows:, :]
        return x

    pad_rows = lax.broadcasted_iota(I32, (LANE, PQB), 0) // HEAD_DIM
    qt = qt_ref[0]
    for hd in range(N_HEADS_A):
        pair = qt[(hd // 2) * LANE:(hd // 2 + 1) * LANE, :].astype(F32)
        qpad[hd] = jnp.where(pad_rows == hd % 2, pair, 0.0).astype(BF16)
    qit = qit_ref[0]
    zeros_idx = jnp.zeros((LANE - IDX_DIM, PQB), BF16)
    for hd in range(IDX_HEADS):
        qipad[hd] = jnp.concatenate([qit[hd * IDX_DIM:(hd + 1) * IDX_DIM, :], zeros_idx], axis=0)

    w_rows = [wt_ref[0, hd:hd + 1, :] * IDX_SCALE for hd in range(IDX_HEADS)]

    def score_chunk(c, carry):
        kc = keys(tailb_ref, c, slice(None))
        s = jnp.zeros((KC, PQB), F32)
        for hd in range(IDX_HEADS):
            s = s + jnp.maximum(_dot(kc, qipad[hd]), 0.0) * w_rows[hd]
        sc[c] = jnp.where(k_off + c * KC <= q_pos, s, NEG_INF)
        return carry

    lax.fori_loop(0, nc, score_chunk, 0)

    @pl.when(nc % 2 == 1)
    def _():
        sc[nc] = jnp.full((KC, PQB), NEG_INF, F32)

    def count(cmp):
        n_acc = 4

        def body(cp, accs):
            accs = list(accs)
            for half in range(2):
                for r in range(KC // 8):
                    x = sc[2 * cp + half, r * 8:(r + 1) * 8, :]
                    accs[r % n_acc] = accs[r % n_acc] + jnp.where(cmp(x), 1.0, 0.0)
            return tuple(accs)
        accs = lax.fori_loop(0, (nc + 1) // 2, body, tuple(jnp.zeros((8, PQB), F32) for _ in range(n_acc)))
        return jnp.sum(sum(accs[1:], accs[0]), axis=0, keepdims=True)

    thr = _kth_largest(lambda cand: count(lambda x: x >= cand), topk, (1, PQB))

    need = float(topk) - count(lambda x: x > thr)
    tri = jnp.where(lax.broadcasted_iota(I32, (KC, KC), 0) >= lax.broadcasted_iota(I32, (KC, KC), 1),
                    1.0, 0.0).astype(BF16)

    def mask_chunk(c, off):
        x = sc[c]
        eq = x == thr
        pre = off + _dot(tri, jnp.where(eq, 1.0, 0.0).astype(BF16))
        sel = ((x > thr) | (eq & (pre <= need))) & (k_off + c * KC <= q_pos)
        sc[c] = jnp.where(sel, 0.0, NEG_INF)
        return pre[KC - 1:KC, :]

    lax.fori_loop(0, nc, mask_chunk, jnp.zeros((1, PQB), F32))

    m_s[...] = jnp.full(m_s.shape, NEG_INF, F32)
    l_s[...] = jnp.zeros(l_s.shape, F32)
    acc_s[...] = jnp.zeros(acc_s.shape, F32)

    def attn_chunk(c, carry, near):
        for sb in range(KC // ATT_SUB):
            k0 = pl.multiple_of(c * KC + sb * ATT_SUB, ATT_SUB)
            mk = sc[c, sb * ATT_SUB:(sb + 1) * ATT_SUB, :]

            def logits(hd):
                pair = slice((hd // 2) * LANE, (hd // 2 + 1) * LANE)
                x = _dot(kb_ref[pl.ds(k0, ATT_SUB), pair], qpad[hd]) + mk
                if near:
                    parts = []
                    for j in range(ATT_SUB // PQB):
                        blk = (KC // PQB) * c + sb * (ATT_SUB // PQB) + j
                        delta = jnp.where(blk == i, dt[hd, PQB:, :], jnp.where(blk == i - 1, dt[hd, :PQB, :], 0.0))
                        parts.append(x[j * PQB:(j + 1) * PQB, :] + delta)
                    x = jnp.concatenate(parts, axis=0)
                return x

            m_fins, alphas = [], []
            for hd in range(N_HEADS_A):
                x = logits(hd)
                lg[hd] = x
                m_old = m_s[hd]
                m_new = jnp.maximum(m_old, jnp.max(x, axis=0, keepdims=True))
                m_fin = jnp.where(m_new == NEG_INF, 0.0, m_new)
                m_fins.append(m_fin)
                alphas.append(jnp.exp(m_old - m_fin))
                m_s[hd] = m_new
            ones_rows = jnp.ones((16, ATT_SUB), BF16)
            for hd in range(N_HEADS_A):
                p = jnp.exp(lg[hd] - m_fins[hd]).astype(BF16)
                vt_h = vt_ref[c, hd * HEAD_DIM:(hd + 1) * HEAD_DIM, sb * ATT_SUB:(sb + 1) * ATT_SUB]
                pv = _dot(jnp.concatenate([vt_h, ones_rows], axis=0), p)
                acc_s[hd] = alphas[hd] * acc_s[hd] + pv[:HEAD_DIM]
                l_s[hd] = alphas[hd] * l_s[hd] + pv[HEAD_DIM:HEAD_DIM + 1]
        return carry

    n_far = jnp.maximum(nc - 2, 0)
    lax.fori_loop(0, n_far, functools.partial(attn_chunk, near=False), 0)
    lax.fori_loop(n_far, nc, functools.partial(attn_chunk, near=True), 0)
    out_t = jnp.concatenate([acc_s[hd] / l_s[hd] for hd in range(N_HEADS_A)], axis=0)
    o_ref[...] = out_t.T


def _prompt_attn(btab, rel_bias, qt, qit, wt, tailb, kb, vt, batch, seq):
    nq = seq // PQB
    ncb = seq // KC
    per = KC // PQB
    topk = min(TOPK_MAX, seq // 4)
    qtile = lambda rows: pl.BlockSpec((1, rows, PQB), lambda b, i: (b * ncb + i // per, 0, i % per))
    return pl.pallas_call(
        functools.partial(_prompt_attn_kernel, topk=topk),
        grid=(batch, nq),
        in_specs=[pl.BlockSpec((2 * PQB, PQB), lambda b, i: (0, 0)),
                  pl.BlockSpec(memory_space=pltpu.SMEM),
                  qtile(A_WIDTH), qtile(IDX_HEADS * IDX_DIM), qtile(8),
                  _resident((seq, LANE), lambda b, i: (b, 0)),
                  _resident((seq, A_WIDTH), lambda b, i: (b, 0)),
                  _resident((ncb, A_WIDTH, KC), lambda b, i: (b, 0, 0))],
        out_specs=pl.BlockSpec((PQB, A_WIDTH), lambda b, i: (b * nq + i, 0)),
        out_shape=jax.ShapeDtypeStruct((batch * seq, A_WIDTH), F32),
        scratch_shapes=[pltpu.VMEM((ncb, KC, PQB), F32),
                        pltpu.VMEM((N_HEADS_A, 2 * PQB, PQB), F32),
                        pltpu.VMEM((N_HEADS_A, LANE, PQB), BF16),
                        pltpu.VMEM((IDX_HEADS, LANE, PQB), BF16),
                        pltpu.VMEM((N_HEADS_A, 1, PQB), F32),
                        pltpu.VMEM((N_HEADS_A, 1, PQB), F32),
                        pltpu.VMEM((N_HEADS_A, HEAD_DIM, PQB), F32),
                        pltpu.VMEM((N_HEADS_A, ATT_SUB, PQB), F32)],
        compiler_params=_cparams(2),
        name="prompt_attn",
    )(btab, rel_bias, qt, qit, wt, tailb, kb, vt)


SEL_PAGES = 32
ATT_PAGES = 32


def _sample_select_kernel(pt_ref, qi_ref, tail_ref, *rest, n_groups, t_new, topk):
    pages = rest[:SEL_PAGES]
    mpast_ref, mnew_ref, scv, scn = rest[SEL_PAGES:]
    g = pl.program_id(1)
    gw = SEL_PAGES * PAGE
    w4 = tail_ref[:, 64:68] * IDX_SCALE
    q4 = qi_ref[...].reshape(IDX_HEADS * t_new, IDX_DIM)

    def scores(d):
        s = jnp.zeros((t_new, d.shape[1]), F32)
        for hd in range(IDX_HEADS):
            s = s + jnp.maximum(d[hd * t_new:(hd + 1) * t_new, :], 0.0) * w4[:, hd:hd + 1]
        return s

    kpast_t = jnp.concatenate([p[0] for p in pages], axis=1).astype(BF16)
    scv[g] = scores(_dot(q4, kpast_t))

    @pl.when(g == n_groups - 1)
    def _():
        rown = lax.broadcasted_iota(I32, (t_new, LANE), 0)
        lanen = lax.broadcasted_iota(I32, (t_new, LANE), 1)
        knew = jnp.concatenate([tail_ref[:, 0:64], jnp.zeros((LANE - t_new, 64), F32)], axis=0).astype(BF16)
        sn = scores(_dot_t(q4, knew))
        causal_n = lanen <= rown
        scn[...] = jnp.where(causal_n, sn, NEG_INF)

        def count(cmp):
            parts = [jnp.where(cmp(scn[...]), 1.0, 0.0)]
            for gg in range(n_groups):
                hit = jnp.where(cmp(scv[gg]), 1.0, 0.0)
                parts.extend(hit[:, j * LANE:(j + 1) * LANE] for j in range(gw // LANE))
            while len(parts) > 1:
                parts = [a + b for a, b in zip(parts[::2], parts[1::2])] + parts[len(parts) & ~1:]
            return jnp.sum(parts[0], axis=-1, keepdims=True)

        thr = _kth_largest(lambda cand: count(lambda x: x >= cand), topk, (t_new, 1))
        need = float(topk) - count(lambda x: x > thr)
        tri, ones = _tri_ones()
        blocks = [(gg, j) for gg in range(n_groups) for j in range(gw // LANE)]
        tile = lambda gg, j: scv[gg, :, j * LANE:(j + 1) * LANE]
        ties = lambda x: jnp.where(x == thr, 1.0, 0.0).astype(BF16)
        off = jnp.zeros((t_new, LANE), F32)
        for gg, j in blocks:
            x = tile(gg, j)
            eqb = ties(x)
            sel = (x > thr) | ((x == thr) & (off + _dot(eqb, tri) <= need))
            mpast_ref[0, :, gg * gw + j * LANE:gg * gw + (j + 1) * LANE] = jnp.where(sel, 0.0, NEG_INF)
            off = off + jnp.sum(eqb.astype(F32), axis=-1, keepdims=True)
        x = scn[...]
        sel = ((x > thr) | ((x == thr) & (off + _dot(ties(x), tri) <= need))) & causal_n
        mnew_ref[0] = jnp.where(sel, 0.0, NEG_INF)


def _sample_select(page_table, qi_hm, tail, cache_idx_t, t_new):
    bsz, n_pages = page_table.shape
    n_groups = n_pages // SEL_PAGES
    past = n_pages * PAGE
    topk = min(TOPK_MAX, (past + t_new) // 4)

    def page_spec(j):
        return pl.BlockSpec((1, IDX_DIM, PAGE), lambda b, g, pt: (pt[b, g * SEL_PAGES + j], 0, 0))

    grid_spec = pltpu.PrefetchScalarGridSpec(
        num_scalar_prefetch=1,
        grid=(bsz, n_groups),
        in_specs=[pl.BlockSpec((IDX_HEADS, t_new, 64), lambda b, g, pt: (0, b, 0)),
                  pl.BlockSpec((t_new, 128), lambda b, g, pt: (b, 0))]
                 + [page_spec(j) for j in range(SEL_PAGES)],
        out_specs=[pl.BlockSpec((1, t_new, past), lambda b, g, pt: (b, 0, 0)),
                   pl.BlockSpec((1, t_new, LANE), lambda b, g, pt: (b, 0, 0))],
        scratch_shapes=[pltpu.VMEM((n_groups, t_new, SEL_PAGES * PAGE), F32),
                        pltpu.VMEM((t_new, LANE), F32)])
    return pl.pallas_call(
        functools.partial(_sample_select_kernel, n_groups=n_groups, t_new=t_new, topk=topk),
        grid_spec=grid_spec,
        out_shape=[jax.ShapeDtypeStruct((bsz, t_new, past), F32),
                   jax.ShapeDtypeStruct((bsz, t_new, LANE), F32)],
        compiler_params=_cparams(2),
        name="sample_select",
    )(page_table, qi_hm, tail, *([cache_idx_t] * SEL_PAGES))


def _sample_attn_kernel(pt_ref, btab_ref, rb_ref, q_ref, knew_ref, vnew_ref, mpast_ref, mnew_ref, *rest,
                        n_steps, t_new):
    kpages = rest[:ATT_PAGES]
    vpages = rest[ATT_PAGES:2 * ATT_PAGES]
    o_ref, qbd, dq, m_s, l_s, acc_s = rest[2 * ATT_PAGES:]
    g = pl.program_id(1)
    hq = N_HEADS_A * t_new
    gw = ATT_PAGES * PAGE
    head_of_row = lax.broadcasted_iota(I32, (hq, A_WIDTH), 0) // t_new
    head_of_col = lax.broadcasted_iota(I32, (hq, A_WIDTH), 1) // HEAD_DIM
    own = head_of_row == head_of_col

    @pl.when((pl.program_id(0) == 0) & (g == 0))
    def _():
        bt = btab_ref[0:t_new, :]
        for hd in range(N_HEADS_A):
            dq[hd * t_new:(hd + 1) * t_new, :] = _bias_delta(bt, rb_ref, hd)

    @pl.when(g == 0)
    def _():
        qt = jnp.concatenate([q_ref[...].astype(F32)] * N_HEADS_A, axis=0)
        qbd[...] = jnp.where(own, qt, 0.0).astype(BF16)
        m_s[...] = jnp.full((hq, 1), NEG_INF, F32)
        l_s[...] = jnp.zeros((hq, 1), F32)
        acc_s[...] = jnp.zeros((hq, A_WIDTH), F32)

    def online(x, vb_t):
        m_old = m_s[...]
        m_new = jnp.maximum(m_old, jnp.max(x, axis=-1, keepdims=True))
        m_fin = jnp.where(m_new == NEG_INF, 0.0, m_new)
        alpha = jnp.exp(m_old - m_fin)
        p = jnp.exp(x - m_fin)
        l_s[...] = alpha * l_s[...] + jnp.sum(p, axis=-1, keepdims=True)
        acc_s[...] = alpha * acc_s[...] + _dot_t(p.astype(BF16), vb_t)
        m_s[...] = m_new

    kb_t = jnp.concatenate([p[0] for p in kpages], axis=1).astype(BF16)
    vb_t = jnp.concatenate([p[0] for p in vpages], axis=1).astype(BF16)
    x = _dot(qbd[...], kb_t) + jnp.concatenate([mpast_ref[0]] * N_HEADS_A, axis=0)
    lane = lax.broadcasted_iota(I32, (hq, gw), 1)
    near = jnp.concatenate([jnp.zeros((hq, gw - LANE), F32), dq[:, :LANE]], axis=1)
    x = x + jnp.where(g == n_steps - 1, near, 0.0)

    online(x, vb_t)

    @pl.when(g == n_steps - 1)
    def _():
        pad = jnp.zeros((LANE - t_new, A_WIDTH), F32)
        kn = jnp.concatenate([knew_ref[...], pad], axis=0).astype(BF16)
        vn = jnp.concatenate([vnew_ref[...], pad], axis=0)
        xn = _dot_t(qbd[...], kn) + jnp.concatenate([mnew_ref[0]] * N_HEADS_A, axis=0) + dq[:, LANE:]
        online(xn, vn.T.astype(BF16))
        res = jnp.where(own, acc_s[...] / l_s[...], 0.0)
        out = res[0:t_new, :]
        for hd in range(1, N_HEADS_A):
            out = out + res[hd * t_new:(hd + 1) * t_new, :]
        o_ref[...] = out


def _sample_attn(page_table, btab, rel_bias, q_rows, k_new, v_new, mpast, mnew, cache_kt, cache_vt, t_new):
    bsz, n_pages = page_table.shape
    n_steps = n_pages // ATT_PAGES
    hq = N_HEADS_A * t_new

    def page_spec(j):
        return pl.BlockSpec((1, A_WIDTH, PAGE), lambda b, g, pt: (pt[b, g * ATT_PAGES + j], 0, 0))

    grid_spec = pltpu.PrefetchScalarGridSpec(
        num_scalar_prefetch=1,
        grid=(bsz, n_steps),
        in_specs=[pl.BlockSpec((QB, 2 * QB), lambda b, g, pt: (0, 0)),
                  pl.BlockSpec(memory_space=pltpu.SMEM),
                  pl.BlockSpec((t_new, A_WIDTH), lambda b, g, pt: (b, 0)),
                  pl.BlockSpec((t_new, A_WIDTH), lambda b, g, pt: (b, 0)),
                  pl.BlockSpec((t_new, A_WIDTH), lambda b, g, pt: (b, 0)),
                  pl.BlockSpec((1, t_new, ATT_PAGES * PAGE), lambda b, g, pt: (b, 0, g)),
                  pl.BlockSpec((1, t_new, LANE), lambda b, g, pt: (b, 0, 0))]
                 + [page_spec(j) for j in range(ATT_PAGES)] * 2,
        out_specs=pl.BlockSpec((t_new, A_WIDTH), lambda b, g, pt: (b, 0)),
        scratch_shapes=[pltpu.VMEM((hq, A_WIDTH), BF16),
                        pltpu.VMEM((hq, 2 * QB), F32),
                        pltpu.VMEM((hq, 1), F32),
                        pltpu.VMEM((hq, 1), F32),
                        pltpu.VMEM((hq, A_WIDTH), F32)])
    return pl.pallas_call(
        functools.partial(_sample_attn_kernel, n_steps=n_steps, t_new=t_new),
        grid_spec=grid_spec,
        out_shape=jax.ShapeDtypeStruct((bsz * t_new, A_WIDTH), F32),
        compiler_params=_cparams(2),
        name="sample_attn",
    )(page_table, btab, rel_bias, q_rows, k_new, v_new, mpast, mnew,
      *([cache_kt] * ATT_PAGES), *([cache_vt] * ATT_PAGES))


def _head_block_ones():
    r = lax.broadcasted_iota(I32, (B_WIDTH, B_WIDTH), 0) // HEAD_B
    c = lax.broadcasted_iota(I32, (B_WIDTH, B_WIDTH), 1) // HEAD_B
    return jnp.where(r == c, 1.0, 0.0).astype(BF16)


def _rwkv_prep_kernel(rw_ref, prev8_ref, prow_ref, mu_ref, w0_ref, a0_ref, kk_ref, ka_ref, rk_ref,
                      w2_ref, a2_ref, g2_ref,
                      r_ref, w_ref, k_ref, v_ref, na_ref, bb_ref, bonus_ref, g_ref, *, tt):
    t = pl.program_id(1)
    cur = rw_ref[0]
    first_prev = jnp.where(t == 0, prow_ref[0], prev8_ref[0, 7:8, :])
    rows = lax.broadcasted_iota(I32, cur.shape, 0)
    prev = jnp.where(rows == 0, first_prev, pltpu.roll(cur, 1, 0))
    mixed = cur + (prev - cur) * mu_ref[...]
    r = mixed[:, 0:512]
    k = mixed[:, 512:1024]
    v = mixed[:, 1024:1536]
    wd = mixed[:, 1536:1600]
    ad = mixed[:, 1600:1664]
    gd = mixed[:, 1664:1792]
    z = w0_ref[...] + _dot(jnp.tanh(wd).astype(BF16), w2_ref[...])
    u = -z
    softplus = jnp.maximum(u, 0.0) + jnp.log1p(jnp.exp(-jnp.abs(u)))
    decay = jnp.exp(-jnp.exp(-softplus - 0.5))
    a = jax.nn.sigmoid(a0_ref[...] + _dot(ad.astype(BF16), a2_ref[...]))
    g = _dot(jax.nn.sigmoid(gd).astype(BF16), g2_ref[...])
    bd = _head_block_ones()
    kk = k * kk_ref[...]
    nrm = jnp.sqrt(_split_dot(kk * kk, bd))
    kk = kk / jnp.maximum(nrm, 1e-12)
    k2 = k * (1.0 + (a - 1.0) * ka_ref[...])
    r_ref[0] = r
    w_ref[0] = decay
    k_ref[0] = k2
    v_ref[0] = v
    na_ref[0] = -kk
    bb_ref[0] = kk * a
    bonus_ref[0] = _split_dot(r * k2 * rk_ref[...], bd) * v
    g_ref[0] = g


def _rwkv_prep(rw3, prow, p, tt):
    bsz, t_len, _ = rw3.shape
    vec = lambda c: pl.BlockSpec((1, c), lambda b, t: (0, 0))
    full = lambda r, c: pl.BlockSpec((r, c), lambda b, t: (0, 0))
    tile = pl.BlockSpec((1, tt, B_WIDTH), lambda b, t: (b, t, 0))
    return pl.pallas_call(
        functools.partial(_rwkv_prep_kernel, tt=tt),
        grid=(bsz, t_len // tt),
        in_specs=[pl.BlockSpec((1, tt, RWKV_COLS), lambda b, t: (b, t, 0)),
                  pl.BlockSpec((1, 8, RWKV_COLS), lambda b, t: (b, jnp.maximum(t * (tt // 8) - 1, 0), 0)),
                  pl.BlockSpec((1, 1, RWKV_COLS), lambda b, t: (b, 0, 0)),
                  vec(RWKV_COLS), vec(B_WIDTH), vec(B_WIDTH), vec(B_WIDTH), vec(B_WIDTH), vec(B_WIDTH),
                  full(D_DECAY, B_WIDTH), full(D_AAA, B_WIDTH), full(D_GATE, B_WIDTH)],
        out_specs=[tile] * 8,
        out_shape=[jax.ShapeDtypeStruct((bsz, t_len, B_WIDTH), F32)] * 8,
        compiler_params=_cparams(2),
        name="rwkv_prep",
    )(rw3, rw3, prow.reshape(bsz, 1, RWKV_COLS), p["mu"], p["w0"], p["a0"], p["k_k"], p["k_a"], p["r_k"],
      p["w2"], p["a2"], p["g2"])


SCAN_BLOCK = 2


def _rwkv_scan_kernel(r_ref, w_ref, k_ref, v_ref, na_ref, bb_ref, s0_ref, y_ref, st_ref, *, bb_n, tc):
    @pl.when(pl.program_id(1) == 0)
    def _():
        st_ref[...] = s0_ref[...]

    lane = lax.broadcasted_iota(I32, (HEAD_B, LANE), 1)
    rowi = lax.broadcasted_iota(I32, (HEAD_B, LANE), 0)
    diag = (lane & (HEAD_B - 1)) == rowi
    step_lane = lane & (HEAD_B - 1)
    bj = lax.broadcasted_iota(I32, (2 * LANE, LANE), 0)
    bl = lax.broadcasted_iota(I32, (2 * LANE, LANE), 1)
    bd2 = jnp.where((bj & (LANE - 1)) // HEAD_B == bl // HEAD_B, 1.0, 0.0).astype(BF16)
    bd1 = bd2[0:LANE]

    def hi_lo(x):
        hi = x.astype(BF16)
        return hi, (x - hi.astype(F32)).astype(BF16)

    chains = [(bi, pr) for bi in range(bb_n) for pr in range(N_HEADS_B // 2)]

    def tile(t8, carry):
        t0 = pl.multiple_of(t8 * 8, 8)
        ins = {c: tuple(ref[c[0], pl.ds(t0, 8), c[1] * LANE:(c[1] + 1) * LANE]
                        for ref in (r_ref, w_ref, k_ref, v_ref, na_ref, bb_ref)) for c in chains}
        state = {c: st_ref[c[0], c[1]] for c in chains}
        ycols = {c: jnp.zeros((HEAD_B, LANE), F32) for c in chains}
        nch = len(chains)
        piece = lambda x, j: x[j * HEAD_B:(j + 1) * HEAD_B]

        def shift_up(x, d):
            return pltpu.roll(x, 8 - d, 0)

        prods = []
        for d in range(1, SCAN_BLOCK):
            for n in (5, 2):
                for c in chains:
                    x = ins[c][n] * shift_up(ins[c][4], d)
                    for u in range(1, d):
                        x = x * shift_up(ins[c][1], u)
                    prods.append(x)
        if prods:
            dots = _dot(jnp.concatenate(hi_lo(jnp.concatenate(prods, axis=0)), axis=1), bd2)

        def coef(d, which, j, l):
            row = (((d - 1) * 2 + which) * nch + j) * 8 + l
            return dots[row:row + 1, :]

        for i0 in range(0, 8, SCAN_BLOCK):
            steps = range(i0, i0 + SCAN_BLOCK)
            row = lambda c, n, i: ins[c][n][i:i + 1, :]
            p_all = []
            for i in steps:
                for c in chains:
                    at = row(c, 4, i)
                    for u in range(i0, i):
                        at = at * row(c, 1, u)
                    p_all.append(jnp.concatenate(hi_lo(state[c] * at), axis=1))
            sab = _dot(jnp.concatenate(p_all, axis=0), bd2)
            z_all = [jnp.where(diag, row(c, 3, i), 0.0).astype(BF16) for i in steps for c in chains]
            vcs = _dot(jnp.concatenate(z_all, axis=0), bd1)
            y_all = []
            for j, c in enumerate(chains):
                s = state[c]
                sas = []
                for m, i in enumerate(steps):
                    sa = piece(sab, m * nch + j)
                    for ml in range(m):
                        sa = sa + sas[ml] * coef(m - ml, 0, j, i0 + ml) \
                            + piece(vcs, ml * nch + j) * coef(m - ml, 1, j, i0 + ml)
                    sas.append(sa)
                    s = s * row(c, 1, i) + sa * row(c, 5, i) + piece(vcs, m * nch + j) * row(c, 2, i)
                    y_all.append((s * row(c, 0, i)).astype(BF16))
                state[c] = s
            yb_all = _dot(jnp.concatenate(y_all, axis=0), bd1)
            for j, c in enumerate(chains):
                for m, i in enumerate(steps):
                    ycols[c] = jnp.where(step_lane == i, piece(yb_all, j * SCAN_BLOCK + m), ycols[c])
        for c in chains:
            st_ref[c[0], c[1]] = state[c]
            yt = ycols[c].T
            y_ref[c[0], pl.ds(t0, 8), c[1] * LANE:c[1] * LANE + HEAD_B] = yt[0:8]
            y_ref[c[0], pl.ds(t0, 8), c[1] * LANE + HEAD_B:(c[1] + 1) * LANE] = yt[HEAD_B:HEAD_B + 8]
        return carry

    lax.fori_loop(0, tc // 8, tile, 0)


def _pair_pack(state):
    b = state.shape[0]
    s = state.reshape(b, N_HEADS_B // 2, 2, HEAD_B, HEAD_B)
    return jnp.transpose(s, (0, 1, 3, 2, 4)).reshape(b, N_HEADS_B // 2, HEAD_B, 2 * HEAD_B)


def _pair_unpack(state):
    b = state.shape[0]
    s = state.reshape(b, N_HEADS_B // 2, HEAD_B, 2, HEAD_B)
    return jnp.transpose(s, (0, 1, 3, 2, 4)).reshape(b, N_HEADS_B, HEAD_B, HEAD_B)


def _rwkv_scan(r, w, k, v, na, bb, state0, bb_n, tc):
    bsz, t_len, _ = r.shape
    tile = pl.BlockSpec((bb_n, tc, B_WIDTH), lambda b, t: (b, t, 0))
    st = pl.BlockSpec((bb_n, N_HEADS_B // 2, HEAD_B, 2 * HEAD_B), lambda b, t: (b, 0, 0, 0))
    y, state = pl.pallas_call(
        functools.partial(_rwkv_scan_kernel, bb_n=bb_n, tc=tc),
        grid=(bsz // bb_n, t_len // tc),
        in_specs=[tile] * 6 + [st],
        out_specs=[tile, st],
        out_shape=[jax.ShapeDtypeStruct((bsz, t_len, B_WIDTH), F32),
                   jax.ShapeDtypeStruct((bsz, N_HEADS_B // 2, HEAD_B, 2 * HEAD_B), F32)],
        compiler_params=_cparams(2),
        name="rwkv_scan",
    )(r, w, k, v, na, bb, _pair_pack(state0))
    return y, _pair_unpack(state)


def _tail_kernel(x_ref, att_ref, y_ref, bonus_ref, g_ref, gate_ref, lnw_ref, lnb_ref, nm_ref, nf_ref,
                 wa_ref, wb_ref, wo_ref, w1_ref, w2_ref, o_ref):
    bd = _head_block_ones()
    y = y_ref[...]
    mean = _split_dot(y, bd) * (1.0 / HEAD_B)
    yc = y - mean
    var = _split_dot(yc * yc, bd) * (1.0 / HEAD_B)
    yn = yc * lax.rsqrt(var + GN_EPS) * lnw_ref[...] + lnb_ref[...]
    rwkv = (yn + bonus_ref[...]) * g_ref[...]
    ga = gate_ref[:, 0:D_MODEL]
    gb = gate_ref[:, D_MODEL:2 * D_MODEL]
    merged = (jax.nn.sigmoid(ga) * _dot(att_ref[...].astype(BF16), wa_ref[...])
              + jax.nn.sigmoid(gb) * _dot(rwkv.astype(BF16), wb_ref[...]))
    x1 = x_ref[...] + _dot(merged.astype(BF16), wo_ref[...])
    hm = _rms(x1, nm_ref[...]).astype(BF16)
    up = jnp.maximum(_dot(hm, w1_ref[...]), 0.0)
    x2 = x1 + _dot((up * up).astype(BF16), w2_ref[...])
    o_ref[...] = _rms(x2, nf_ref[...])


def _tail(x2d, att, y, bonus, g, gates, p, tm):
    n = x2d.shape[0]
    row = lambda i: (i, 0)
    vec = lambda c: pl.BlockSpec((1, c), lambda i: (0, 0))
    res = lambda r, c: _resident((r, c), lambda i: (0, 0))
    return pl.pallas_call(
        _tail_kernel,
        grid=(n // tm,),
        in_specs=[pl.BlockSpec((tm, D_MODEL), row),
                  pl.BlockSpec((tm, A_WIDTH), row), pl.BlockSpec((tm, B_WIDTH), row),
                  pl.BlockSpec((tm, B_WIDTH), row), pl.BlockSpec((tm, B_WIDTH), row),
                  pl.BlockSpec((tm, 2 * D_MODEL), row),
                  vec(B_WIDTH), vec(B_WIDTH), vec(D_MODEL), vec(D_MODEL),
                  res(A_WIDTH, D_MODEL), res(B_WIDTH, D_MODEL), res(D_MODEL, D_MODEL),
                  res(D_MODEL, D_FF), res(D_FF, D_MODEL)],
        out_specs=pl.BlockSpec((tm, D_MODEL), row),
        out_shape=jax.ShapeDtypeStruct((n, D_MODEL), F32),
        compiler_params=_cparams(1),
        name="tail",
    )(x2d, att, y, bonus, g, gates, p["ln_w"], p["ln_b"], p["norm_mlp"], p["norm_final"],
      p["wa"], p["wb"], p["wo"], p["w1"], p["w2m"])


def _prepare_params(norm_mix, w_in, rwkv_mu, rwkv_w0, rwkv_w2, rwkv_a0, rwkv_a2, rwkv_g2, rwkv_k_k, rwkv_k_a,
                    rwkv_r_k, rwkv_ln_w, rwkv_ln_b, w_branch_a, w_branch_b, w_out, norm_mlp, w_mlp_in,
                    w_mlp_out, norm_final):
    wb16 = w_in.astype(BF16)
    o_idx = 3 * A_WIDTH
    o_kidx = o_idx + IDX_HEADS * IDX_DIM
    o_widx = o_kidx + IDX_DIM
    pad = jnp.zeros((D_MODEL, 128 - IDX_DIM - IDX_HEADS), BF16)
    w_attn = jnp.concatenate([wb16[:, :o_kidx], wb16[:, o_kidx:o_widx + IDX_HEADS], pad], axis=1)
    o = RWKV_OFF
    def regroup(a):
        return jnp.concatenate([a[..., 0:512], a[..., 576:1600], a[..., 512:576], a[..., 1600:1792]], axis=-1)
    w_rw = regroup(wb16[:, o:o + RWKV_COLS])
    w_rest = jnp.concatenate([w_rw, wb16[:, o + RWKV_COLS:]], axis=1)
    r1 = lambda a: a.reshape(1, -1).astype(F32)
    return dict(
        norm_mix=r1(norm_mix), w_attn=w_attn, w_rest=w_rest, w_rw=w_rw,
        mu=r1(regroup(rwkv_mu)), w0=r1(rwkv_w0), a0=r1(rwkv_a0), k_k=r1(rwkv_k_k), k_a=r1(rwkv_k_a),
        r_k=r1(rwkv_r_k), w2=rwkv_w2.astype(BF16), a2=rwkv_a2.astype(BF16), g2=rwkv_g2.astype(BF16),
        ln_w=r1(rwkv_ln_w), ln_b=r1(rwkv_ln_b), norm_mlp=r1(norm_mlp), norm_final=r1(norm_final),
        wa=w_branch_a.astype(BF16), wb=w_branch_b.astype(BF16), wo=w_out.astype(BF16),
        w1=w_mlp_in.astype(BF16), w2m=w_mlp_out.astype(BF16))


def _layer(x, shift_rows, wkv0, attend, p, tm, tm_tail, scan_tc):
    bsz, t_len, _ = x.shape
    n = bsz * t_len
    x2d = x.reshape(n, D_MODEL)
    hl_rows = 8 if t_len >= tm else tm
    names = ("q_hm", "k", "v", "qt", "kb", "vt", "qi_hm", "qit", "tail", "tailb", "wt", "hlast")
    long_seq = t_len >= tm
    pr = dict(zip(names, _proj_attn(x2d, p["norm_mix"], p["w_attn"], tm, hl_rows, t_len // tm if long_seq else 0)))
    tail, hlast = pr["tail"], pr["hlast"]
    if long_seq:
        heads = lambda a: jnp.transpose(a.reshape(bsz, N_HEADS_A, HEAD_DIM, t_len), (0, 3, 1, 2))
    else:
        heads = lambda a: a.reshape(bsz, t_len, N_HEADS_A, HEAD_DIM)
    rw, gates = _proj_rest(x2d, p["norm_mix"], p["w_rest"], tm)
    att = attend(pr)
    prow = _shift_proj(shift_rows, p["w_rw"])
    r, w, k2, v2, na, bb, bonus, g = _rwkv_prep(rw.reshape(bsz, t_len, RWKV_COLS), prow, p, min(256, t_len))
    y, wkv = _rwkv_scan(r, w, k2, v2, na, bb, wkv0, 2, scan_tc)
    flat = lambda a: a.reshape(n, B_WIDTH)
    out = _tail(x2d, att, flat(y), flat(bonus), flat(g), gates, p, tm_tail)
    if long_seq:
        shift = hlast.reshape(bsz, t_len // tm, 8, D_MODEL)[:, -1, 7]
    else:
        shift = hlast.reshape(bsz, t_len, D_MODEL)[:, -1]
    return (out.reshape(bsz, t_len, D_MODEL), heads(pr["k"]), heads(pr["v"]),
            tail[:, :IDX_DIM].reshape(bsz, t_len, IDX_DIM), wkv, shift)


def kernel(x_prompt, x_sample, cache_k, cache_v, cache_idx_k, state_wkv, state_shift, page_table, rel_bias, norm_mix, w_in, rwkv_mu, rwkv_w0, rwkv_w2, rwkv_a0, rwkv_a2, rwkv_g2, rwkv_k_k, rwkv_k_a, rwkv_r_k, rwkv_ln_w, rwkv_ln_b, w_branch_a, w_branch_b, w_out, norm_mlp, w_mlp_in, w_mlp_out, norm_final):
    p = _prepare_params(norm_mix, w_in, rwkv_mu, rwkv_w0, rwkv_w2, rwkv_a0, rwkv_a2, rwkv_g2, rwkv_k_k,
                        rwkv_k_a, rwkv_r_k, rwkv_ln_w, rwkv_ln_b, w_branch_a, w_branch_b, w_out, norm_mlp,
                        w_mlp_in, w_mlp_out, norm_final)
    btab = jnp.asarray(_window_buckets())
    btab_t = jnp.asarray(_window_buckets_t())
    b_p, s_p, _ = x_prompt.shape
    b_s, t_s, _ = x_sample.shape
    n_pool = cache_k.shape[0]

    def prompt_attend(pr):
        return _prompt_attn(btab_t, rel_bias, pr["qt"], pr["qit"], pr["wt"], pr["tailb"], pr["kb"], pr["vt"],
                            b_p, s_p)

    def sample_attend(pr):
        key_minor = lambda c: jnp.transpose(c.reshape(n_pool, PAGE, -1), (0, 2, 1))
        mpast, mnew = _sample_select(page_table, pr["qi_hm"], pr["tail"], key_minor(cache_idx_k), t_s)
        q_rows = jnp.transpose(pr["q_hm"], (1, 0, 2)).reshape(b_s * t_s, A_WIDTH)
        return _sample_attn(page_table, btab, rel_bias, q_rows, pr["k"], pr["v"], mpast, mnew,
                            key_minor(cache_k), key_minor(cache_v), t_s)

    zero_shift = jnp.zeros((b_p, D_MODEL), F32)
    zero_wkv = jnp.zeros((b_p, N_HEADS_B, HEAD_B, HEAD_B), F32)
    y_p, k_p, v_p, ik_p, wkv_p, sh_p = _layer(x_prompt, zero_shift, zero_wkv, prompt_attend, p, KC, 256, 256)
    y_s, k_s, v_s, ik_s, wkv_s, sh_s = _layer(x_sample, state_shift, state_wkv, sample_attend, p,
                                              b_s * t_s, b_s * t_s, t_s)
    return (y_p, y_s, k_p, v_p, ik_p, wkv_p, sh_p, k_s, v_s, ik_s, wkv_s, sh_s)
```
